```python
import math
import jax, jax.numpy as jnp
from jax import lax
import numpy as np

D_MODEL = 1024
BATCH = 8
SEQ = 4096
DEPTH = 1
DEC_BATCH = 32
DEC_SEQ = 8
PAST_LEN = 16384
PAGE_SIZE = 128

D_MIX = D_MODEL
D_ATTN = D_MIX // 2
D_CONV = D_MIX - D_ATTN
HEAD_DIM = 64
N_HEADS = D_ATTN // HEAD_DIM
N_IDX_HEADS = 8
IDX_DIM = 32
TOPK_MAX = 256
CONV_WIDTH = 3
D_FF = 4 * D_MODEL
N_BUCKETS = 32
MAX_DISTANCE = 128
Q_BLOCK = 128
EPS = 1e-6
ATTN_SCALE = HEAD_DIM ** -0.5
INDEX_SCALE = (IDX_DIM ** -0.5) * (N_IDX_HEADS ** -0.5)
SPLITS = (D_ATTN, D_ATTN, D_ATTN,
          N_IDX_HEADS * IDX_DIM, IDX_DIM,
          N_IDX_HEADS,
          D_CONV, D_CONV, D_CONV)
D_IN = sum(SPLITS)

kernel_name = "hymba_dsa_shortconv_decode_step"


def rmsnorm(x, g):
    xf = x.astype(jnp.float32)
    y = xf * lax.rsqrt(jnp.mean(xf * xf, axis=-1, keepdims=True) + EPS)
    return (y * g.astype(jnp.float32)).astype(x.dtype)


def t5_bucket(dist):
    n = jnp.maximum(dist, 0)
    max_exact = N_BUCKETS // 2
    nf = jnp.maximum(n, 1).astype(jnp.float32)
    large = max_exact + (jnp.log(nf / max_exact) / math.log(MAX_DISTANCE / max_exact)
                         * (N_BUCKETS - max_exact)).astype(jnp.int32)
    large = jnp.minimum(large, N_BUCKETS - 1)
    return jnp.where(n < max_exact, n, large)


def project(xn, w_in):
    b, t = xn.shape[:2]
    z = xn @ w_in
    q, k, v, qi, ki, wi, bg, cg, h = jnp.split(z, list(np.cumsum(SPLITS)[:-1]), axis=-1)
    q = q.reshape(b, t, N_HEADS, HEAD_DIM)
    k = k.reshape(b, t, N_HEADS, HEAD_DIM)
    v = v.reshape(b, t, N_HEADS, HEAD_DIM)
    qi = qi.reshape(b, t, N_IDX_HEADS, IDX_DIM)
    return q, k, v, qi, ki, wi, bg, cg, h


def indexer_scores(qi, wi, ki):
    s = jnp.einsum('bthd,bsd->bths', qi, ki).astype(jnp.float32)
    return jnp.einsum('bths,bth->bts', jax.nn.relu(s), wi.astype(jnp.float32)) * INDEX_SCALE


def select_keys(scores, q_pos, n_sel):
    k_pos = jnp.arange(scores.shape[-1], dtype=jnp.int32)
    scores = jnp.where(k_pos[None, None, :] <= q_pos[None, :, None], scores, -jnp.inf)
    return lax.top_k(scores, n_sel)[1].astype(jnp.int32)


def gather_rows(src, idx):
    return jax.vmap(lambda s_, i_: s_[i_])(src, idx)


def sparse_attend(q, k_sel, v_sel, q_pos, k_pos, rel_bias):
    logits = jnp.einsum('bthd,btkhd->bhtk', q, k_sel).astype(jnp.float32) * ATTN_SCALE
    dist = q_pos[None, :, None] - k_pos
    bias = rel_bias[t5_bucket(dist)].astype(jnp.float32)
    logits = logits + bias.transpose(0, 3, 1, 2)
    logits = jnp.where((dist >= 0)[:, None], logits, -1e30)
    p = jax.nn.softmax(logits, axis=-1).astype(v_sel.dtype)
    out = jnp.einsum('bhtk,btkhd->bthd', p, v_sel)
    return out.reshape(q.shape[0], q.shape[1], D_ATTN)


def prompt_attention(q, k, v, qi, ki, wi, rel_bias):
    b, s = q.shape[:2]
    n_sel = min(TOPK_MAX, s // 4)
    nb = s // Q_BLOCK

    def block(args):
        qb, qib, wib, pos = args
        idx = select_keys(indexer_scores(qib, wib, ki), pos, n_sel)
        return sparse_attend(qb, gather_rows(k, idx), gather_rows(v, idx), pos, idx, rel_bias)

    to_blocks = lambda a: a.reshape(b, nb, Q_BLOCK, *a.shape[2:]).swapaxes(0, 1)
    pos = jnp.arange(s, dtype=jnp.int32).reshape(nb, Q_BLOCK)
    out = lax.map(block, (to_blocks(q), to_blocks(qi), to_blocks(wi), pos))
    return out.swapaxes(0, 1).reshape(b, s, D_ATTN)


def sample_attention(q, k_new, v_new, qi, ki_new, wi, cache_k, cache_v, cache_kidx,
                     page_table, rel_bias):
    db, t = q.shape[:2]
    past = page_table.shape[1] * PAGE_SIZE
    n_sel = min(TOPK_MAX, (past + t) // 4)
    ki_past = cache_kidx[page_table].reshape(db, past, IDX_DIM)
    ki_all = jnp.concatenate([ki_past, ki_new], axis=1)
    q_pos = past + jnp.arange(t, dtype=jnp.int32)
    idx = select_keys(indexer_scores(qi, wi, ki_all), q_pos, n_sel)
    in_past = idx < past
    pidx = jnp.minimum(idx, past - 1)
    phys = jax.vmap(lambda pt, i: pt[i])(page_table, pidx // PAGE_SIZE)
    off = pidx % PAGE_SIZE
    nidx = jnp.clip(idx - past, 0, t - 1)

    def gather(cache, new):
        return jnp.where(in_past[..., None, None], cache[phys, off], gather_rows(new, nidx))

    return sparse_attend(q, gather(cache_k, k_new), gather(cache_v, v_new), q_pos, idx, rel_bias)


def conv_mixer(bg, cg, h, prev, conv_w):
    u = cg * h
    up = jnp.concatenate([prev.astype(u.dtype), u], axis=1)
    t = u.shape[1]
    y = up[:, 0:t] * conv_w[0]
    for j in range(1, CONV_WIDTH):
        y = y + up[:, j:j + t] * conv_w[j]
    return bg * y, up[:, -(CONV_WIDTH - 1):]


def out_and_mlp(x, a, b, w_out, g_mlp, w_up, w_down):
    x = x + jnp.concatenate([a, b], axis=-1) @ w_out
    hn = rmsnorm(x, g_mlp)
    return x + jnp.square(jax.nn.relu(hn @ w_up)) @ w_down


def setup_inputs(seed: int = 0) -> dict:
    key = jax.random.key(seed)
    ks = jax.random.split(key, 20)
    n_pages = PAST_LEN // PAGE_SIZE
    in_use = DEC_BATCH * n_pages
    n_phys = in_use + max(1, in_use // 4)
    f32 = jnp.float32
    nrm = lambda k, shape, s=1.0: (jax.random.normal(k, shape, f32) * s)
    page_table = jax.random.permutation(ks[0], n_phys)[:in_use].reshape(DEC_BATCH, n_pages).astype(jnp.int32)
    return {
        "x_prompt": nrm(ks[1], (BATCH, SEQ, D_MODEL)),
        "x_sample": nrm(ks[2], (DEC_BATCH, DEC_SEQ, D_MODEL)),
        "cache_k": nrm(ks[3], (DEPTH, n_phys, PAGE_SIZE, N_HEADS, HEAD_DIM)),
        "cache_v": nrm(ks[4], (DEPTH, n_phys, PAGE_SIZE, N_HEADS, HEAD_DIM)),
        "cache_kidx": nrm(ks[5], (DEPTH, n_phys, PAGE_SIZE, IDX_DIM)),
        "state_conv": nrm(ks[6], (DEPTH, DEC_BATCH, CONV_WIDTH - 1, D_CONV)),
        "page_table": page_table,
        "rel_bias": nrm(ks[7], (N_BUCKETS, N_HEADS), 0.5),
        "g_mix": 1.0 + nrm(ks[8], (DEPTH, D_MODEL), 0.01),
        "w_in": nrm(ks[9], (DEPTH, D_MODEL, D_IN), D_MODEL ** -0.5),
        "conv_w": nrm(ks[10], (DEPTH, CONV_WIDTH, D_CONV), CONV_WIDTH ** -0.5),
        "w_out": nrm(ks[11], (DEPTH, D_MIX, D_MODEL), D_MIX ** -0.5),
        "g_mlp": 1.0 + nrm(ks[12], (DEPTH, D_MODEL), 0.01),
        "w_up": nrm(ks[13], (DEPTH, D_MODEL, D_FF), D_MODEL ** -0.5),
        "w_down": nrm(ks[14], (DEPTH, D_FF, D_MODEL), D_FF ** -0.5),
        "g_final": 1.0 + nrm(ks[15], (D_MODEL,), 0.01),
    }


def reference(x_prompt, x_sample, cache_k, cache_v, cache_kidx, state_conv, page_table,
              rel_bias, g_mix, w_in, conv_w, w_out, g_mlp, w_up, w_down, g_final):
    xp, xs = x_prompt, x_sample
    kp_l, vp_l, kip_l, cp_l = [], [], [], []
    ks_l, vs_l, kis_l, cs_l = [], [], [], []
    for l in range(DEPTH):
        q, k, v, qi, ki, wi, bg, cg, h = project(rmsnorm(xp, g_mix[l]), w_in[l])
        a = prompt_attention(q, k, v, qi, ki, wi, rel_bias)
        zeros = jnp.zeros((xp.shape[0], CONV_WIDTH - 1, D_CONV), xp.dtype)
        b, conv_p = conv_mixer(bg, cg, h, zeros, conv_w[l])
        xp = out_and_mlp(xp, a, b, w_out[l], g_mlp[l], w_up[l], w_down[l])
        kp_l.append(k); vp_l.append(v); kip_l.append(ki); cp_l.append(conv_p)
        q, k, v, qi, ki, wi, bg, cg, h = project(rmsnorm(xs, g_mix[l]), w_in[l])
        a = sample_attention(q, k, v, qi, ki, wi, cache_k[l], cache_v[l], cache_kidx[l],
                             page_table, rel_bias)
        b, conv_s = conv_mixer(bg, cg, h, state_conv[l], conv_w[l])
        xs = out_and_mlp(xs, a, b, w_out[l], g_mlp[l], w_up[l], w_down[l])
        ks_l.append(k); vs_l.append(v); kis_l.append(ki); cs_l.append(conv_s)
    y_prompt = rmsnorm(xp, g_final)
    y_sample = rmsnorm(xs, g_final)
    k_prompt = jnp.stack(kp_l); v_prompt = jnp.stack(vp_l)
    kidx_prompt = jnp.stack(kip_l); conv_prompt = jnp.stack(cp_l)
    k_sample = jnp.stack(ks_l); v_sample = jnp.stack(vs_l)
    kidx_sample = jnp.stack(kis_l); conv_sample = jnp.stack(cs_l)
    return (y_prompt, y_sample, k_prompt, v_prompt, kidx_prompt, conv_prompt,
            k_sample, v_sample, kidx_sample, conv_sample)
```

```python
import functools
import math

import jax
import jax.numpy as jnp
import numpy as np
from jax import lax
from jax.experimental import pallas as pl
from jax.experimental.pallas import tpu as pltpu

F32 = jnp.float32
BF16 = jnp.bfloat16
I32 = jnp.int32

HEAD_DIM = 64
N_HEADS = 8
N_IDX_HEADS = 8
IDX_DIM = 32
D_ATTN = N_HEADS * HEAD_DIM
TOPK_MAX = 256
CONV_WIDTH = 3
N_BUCKETS = 32
MAX_DISTANCE = 128
EPS = 1e-6
ATTN_SCALE = HEAD_DIM ** -0.5
INDEX_SCALE = (IDX_DIM ** -0.5) * (N_IDX_HEADS ** -0.5)

LANES = 128
NEG = -1e30
VMEM_LIMIT = 56 * 1024 * 1024


def _float_key(v):
    b = int(np.array(v, np.float32).view(np.int32))
    return b if b >= 0 else b ^ 0x7FFFFFFF


KEY_LO = _float_key(-np.finfo(np.float32).max)
KEY_HI = _float_key(np.inf)
N_BISECT = 33


def _key_to_float(k):
    bits = k ^ ((k >> 31) & 0x7FFFFFFF)
    return lax.bitcast_convert_type(bits, F32)


def _mid(lo, hi):
    return (lo >> 1) + (hi >> 1) + (lo & hi & 1)


def _bias_body(rb_ref, o_ref, *, offs, sa, sb):
    w = pl.program_id(0)
    h = pl.program_id(1)
    shape = o_ref.shape
    a = lax.broadcasted_iota(I32, shape, 0)
    b = lax.broadcasted_iota(I32, shape, 1)
    off = jnp.where(w == 0, offs[0], offs[1])
    dist = off + sa * a + sb * b
    n = jnp.maximum(dist, 0)
    max_exact = N_BUCKETS // 2
    nf = jnp.maximum(n, 1).astype(F32)
    large = max_exact + (jnp.log(nf / max_exact) / math.log(MAX_DISTANCE / max_exact)
                         * (N_BUCKETS - max_exact)).astype(I32)
    large = jnp.minimum(large, N_BUCKETS - 1)
    bucket = jnp.where(n < max_exact, n, large)
    val = jnp.zeros(shape, F32)
    for k in range(N_BUCKETS):
        val = jnp.where(bucket == k, rb_ref[k, h], val)
    o_ref[...] = val - rb_ref[N_BUCKETS - 1, h]


def _bias_tables(rel_bias, rows, cols, offs, sa, sb):
    return pl.pallas_call(
        functools.partial(_bias_body, offs=offs, sa=sa, sb=sb),
        grid=(2, N_HEADS),
        in_specs=[pl.BlockSpec(memory_space=pltpu.SMEM)],
        out_specs=pl.BlockSpec((None, None, rows, cols), lambda w, h: (w, h, 0, 0)),
        out_shape=jax.ShapeDtypeStruct((2, N_HEADS, rows, cols), F32),
        name="bias_tables",
    )(rel_bias)


def _far_bucket_is_constant(min_dist, max_dist):
    d = np.arange(min_dist, max_dist + 1, dtype=np.float64)
    me = N_BUCKETS // 2
    b = me + np.floor(np.log(d / me) / math.log(MAX_DISTANCE / me) * (N_BUCKETS - me))
    return bool(np.all(np.minimum(b, N_BUCKETS - 1) == N_BUCKETS - 1)) and min_dist >= me


T_Q, T_K, T_V, T_QI, T_KI, T_WI, T_END = 0, 512, 1024, 1536, 1792, 1824, 1840
N_K, N_BG, N_CG, N_H, N_KI, N_END = 0, 512, 1024, 1536, 2048, 2176


def _inproj_body(x_ref, g_ref, wn_ref, wt_ref,
                 qT_ref, kT_ref, vT_ref, vTb_ref, qiT_ref, kiT_ref, wiT_ref,
                 kn_ref, kin_ref, bg_ref, u_ref):
    x = x_ref[...]
    ms = jnp.mean(x * x, axis=-1, keepdims=True)
    xn = ((x * lax.rsqrt(ms + EPS)) * g_ref[...]).astype(BF16)

    def nat(a, b):
        return jnp.dot(xn, wn_ref[:, a:b], preferred_element_type=F32)

    def tra(a, b):
        return lax.dot_general(wt_ref[a:b, :], xn, (((1,), (1,)), ((), ())),
                               preferred_element_type=F32)

    kn_ref[...] = nat(N_K, N_BG).astype(BF16)
    bg_ref[...] = nat(N_BG, N_CG)
    u_ref[...] = nat(N_CG, N_H) * nat(N_H, N_KI)
    kin_ref[...] = nat(N_KI, N_END).astype(BF16)

    qT_ref[...] = (tra(T_Q, T_K) * ATTN_SCALE).astype(BF16)
    kT_ref[...] = tra(T_K, T_V)
    vt = tra(T_V, T_QI)
    vT_ref[...] = vt
    vTb_ref[...] = vt.astype(BF16)
    qiT_ref[...] = tra(T_QI, T_KI).astype(BF16)
    kiT_ref[...] = tra(T_KI, T_WI)
    wiT_ref[...] = tra(T_WI, T_END)[0:N_IDX_HEADS, :] * INDEX_SCALE


def _in_proj(x2, g, wn, wt, tm):
    n, d = x2.shape
    assert n % tm == 0
    const = lambda i: (0, 0)
    row = lambda i: (i, 0)
    col = lambda i: (0, i)
    outs = [
        ((D_ATTN, n), BF16, (D_ATTN, tm), col),
        ((D_ATTN, n), F32, (D_ATTN, tm), col),
        ((D_ATTN, n), F32, (D_ATTN, tm), col),
        ((D_ATTN, n), BF16, (D_ATTN, tm), col),
        ((N_IDX_HEADS * IDX_DIM, n), BF16, (N_IDX_HEADS * IDX_DIM, tm), col),
        ((IDX_DIM, n), F32, (IDX_DIM, tm), col),
        ((N_IDX_HEADS, n), F32, (N_IDX_HEADS, tm), col),
        ((n, D_ATTN), BF16, (tm, D_ATTN), row),
        ((n, LANES), BF16, (tm, LANES), row),
        ((n, D_ATTN), F32, (tm, D_ATTN), row),
        ((n, D_ATTN), F32, (tm, D_ATTN), row),
    ]
    return pl.pallas_call(
        _inproj_body,
        grid=(n // tm,),
        in_specs=[pl.BlockSpec((tm, d), row),
                  pl.BlockSpec((1, d), const),
                  pl.BlockSpec(wn.shape, const),
                  pl.BlockSpec(wt.shape, const)],
        out_specs=[pl.BlockSpec(bs, im) for (_, _, bs, im) in outs],
        out_shape=[jax.ShapeDtypeStruct(s, dt) for (s, dt, _, _) in outs],
        compiler_params=pltpu.CompilerParams(dimension_semantics=("arbitrary",),
                                             vmem_limit_bytes=VMEM_LIMIT),
        name="in_proj",
    )(x2, g, wn, wt)


def _pattn_body(qT_ref, qiT_ref, wiT_ref, kn_ref, kin_ref, vT_ref, bias_ref, a_ref,
                it_ref, qz_ref, m_ref, l_ref, acc_ref, p_ref, *, tq, n_sel):
    i = pl.program_id(1)
    nch = i + 1
    ck = tq
    s_tot = kn_ref.shape[0]
    q_idx = i * tq + lax.broadcasted_iota(I32, (ck, tq), 1)
    row_iota = lax.broadcasted_iota(I32, (ck, tq), 0)

    def idx_chunk(j, carry):
        r0 = pl.multiple_of(j * ck, ck)
        ki = kin_ref[pl.ds(r0, ck), :][:, 0:IDX_DIM]
        acc = jnp.zeros((ck, tq), F32)
        for h in range(N_IDX_HEADS):
            s = jnp.dot(ki, qiT_ref[h * IDX_DIM:(h + 1) * IDX_DIM, :],
                        preferred_element_type=F32)
            acc = acc + wiT_ref[h:h + 1, :] * jnp.maximum(s, 0.0)
        acc = jnp.where(r0 + row_iota <= q_idx, acc, -jnp.inf)
        it_ref[pl.ds(r0, ck), :] = acc
        return carry

    lax.fori_loop(0, nch, idx_chunk, 0)

    def count(pred):
        def body(j, c8):
            r0 = pl.multiple_of(j * ck, ck)
            x = it_ref[pl.ds(r0, ck), :]
            hit = jnp.where(pred(x, r0), 1.0, 0.0)
            return c8 + hit.reshape(ck // 8, 8, tq).sum(axis=0)
        c8 = lax.fori_loop(0, nch, body, jnp.zeros((8, tq), F32))
        return c8.sum(axis=0, keepdims=True)

    def bis(_, lohi):
        lo, hi = lohi
        mid = _mid(lo, hi)
        midf = _key_to_float(mid)
        ok = count(lambda x, r0: x >= midf) >= n_sel
        return jnp.where(ok, mid, lo), jnp.where(ok, hi, mid)

    lo, _ = lax.fori_loop(0, N_BISECT, bis,
                          (jnp.full((1, tq), KEY_LO, I32), jnp.full((1, tq), KEY_HI, I32)))
    tau = _key_to_float(lo)
    c_ge = count(lambda x, r0: x >= tau)
    c_gt = count(lambda x, r0: x > tau)
    need = n_sel - c_gt

    p_ref[...] = jnp.full((1, tq), s_tot, I32)

    @pl.when(jnp.max(c_ge) > n_sel)
    def _():
        def tie(_, lohi):
            plo, phi = lohi
            pm = (plo + phi) >> 1
            ok = count(lambda x, r0: (x == tau) & (r0 + row_iota <= pm)) >= need
            return jnp.where(ok, plo, pm), jnp.where(ok, pm, phi)
        n_it = int(math.ceil(math.log2(s_tot))) + 1
        _, phi = lax.fori_loop(0, n_it, tie,
                               (jnp.full((1, tq), -1, I32), jnp.full((1, tq), s_tot - 1, I32)))
        p_ref[...] = phi

    pcut = p_ref[...]

    half = lax.broadcasted_iota(I32, (2 * HEAD_DIM, tq), 0) // HEAD_DIM
    for h in range(N_HEADS):
        pair = qT_ref[(h // 2) * 2 * HEAD_DIM:(h // 2 + 1) * 2 * HEAD_DIM, :]
        qz_ref[h] = jnp.where(half == (h % 2), pair, jnp.zeros_like(pair))
    m_ref[...] = jnp.full(m_ref.shape, NEG, F32)
    l_ref[...] = jnp.zeros(l_ref.shape, F32)
    acc_ref[...] = jnp.zeros(acc_ref.shape, F32)

    def attend(j, bias_w):
        r0 = pl.multiple_of(j * ck, ck)
        x = it_ref[pl.ds(r0, ck), :]
        sel = (x > tau) | ((x == tau) & (r0 + row_iota <= pcut))
        for h in range(N_HEADS):
            kp = kn_ref[pl.ds(r0, ck), (h // 2) * 2 * HEAD_DIM:(h // 2 + 1) * 2 * HEAD_DIM]
            s = jnp.dot(kp, qz_ref[h], preferred_element_type=F32)
            if bias_w is not None:
                s = s + bias_ref[bias_w, h]
            s = jnp.where(sel, s, NEG)
            m_old = m_ref[h:h + 1, :]
            m_new = jnp.maximum(m_old, jnp.max(s, axis=0, keepdims=True))
            p = jnp.exp(s - m_new)
            alpha = jnp.exp(m_old - m_new)
            l_ref[h:h + 1, :] = alpha * l_ref[h:h + 1, :] + jnp.sum(p, axis=0, keepdims=True)
            vt = vT_ref[h * HEAD_DIM:(h + 1) * HEAD_DIM, pl.ds(r0, ck)]
            pv = jnp.dot(vt, p.astype(BF16), preferred_element_type=F32)
            acc_ref[h * HEAD_DIM:(h + 1) * HEAD_DIM, :] = (
                alpha * acc_ref[h * HEAD_DIM:(h + 1) * HEAD_DIM, :] + pv)
            m_ref[h:h + 1, :] = m_new

    def far_chunk(j, carry):
        attend(j, None)
        return carry

    lax.fori_loop(0, jnp.maximum(i - 1, 0), far_chunk, 0)

    @pl.when(i >= 1)
    def _():
        attend(i - 1, 1)

    attend(i, 0)

    for h in range(N_HEADS):
        sl = slice(h * HEAD_DIM, (h + 1) * HEAD_DIM)
        acc_ref[sl, :] = acc_ref[sl, :] / l_ref[h:h + 1, :]
    a_ref[...] = acc_ref[...].T.astype(BF16)


def _prompt_attention(qT, qiT, wiT, kn, kin, vTb, bias, batch, seq, tq, n_sel):
    nq = seq // tq
    blk = lambda b, i: (0, b * nq + i)
    return pl.pallas_call(
        functools.partial(_pattn_body, tq=tq, n_sel=n_sel),
        grid=(batch, nq),
        in_specs=[pl.BlockSpec((D_ATTN, tq), blk),
                  pl.BlockSpec((N_IDX_HEADS * IDX_DIM, tq), blk),
                  pl.BlockSpec((N_IDX_HEADS, tq), blk),
                  pl.BlockSpec((seq, D_ATTN), lambda b, i: (b, 0)),
                  pl.BlockSpec((seq, LANES), lambda b, i: (b, 0)),
                  pl.BlockSpec((D_ATTN, seq), lambda b, i: (0, b)),
                  pl.BlockSpec(bias.shape, lambda b, i: (0, 0, 0, 0))],
        out_specs=pl.BlockSpec((tq, D_ATTN), lambda b, i: (b * nq + i, 0)),
        out_shape=jax.ShapeDtypeStruct((batch * seq, D_ATTN), BF16),
        scratch_shapes=[pltpu.VMEM((seq, tq), F32),
                        pltpu.VMEM((N_HEADS, 2 * HEAD_DIM, tq), BF16),
                        pltpu.VMEM((N_HEADS, tq), F32),
                        pltpu.VMEM((N_HEADS, tq), F32),
                        pltpu.VMEM((D_ATTN, tq), F32),
                        pltpu.VMEM((1, tq), I32)],
        compiler_params=pltpu.CompilerParams(dimension_semantics=("arbitrary", "arbitrary"),
                                             vmem_limit_bytes=VMEM_LIMIT),
        name="prompt_attention",
    )(qT, qiT, wiT, kn, kin, vTb, bias)


def _sidx_body(pt_ref, qi_ref, w_ref, kinew_ref, *rest, pg, n_pages, t_new, n_sel):
    pages = rest[:pg]
    madd_ref = rest[pg]
    it_ref = rest[pg + 1]
    g = pl.program_id(1)
    ps = pages[0].shape[-1]
    past = n_pages * ps
    qi = qi_ref[...]
    w = w_ref[...]

    def scores(kt):
        s = jnp.dot(qi, kt.astype(BF16), preferred_element_type=F32)
        r = jnp.maximum(s, 0.0) * w
        return r.reshape(N_IDX_HEADS, t_new, kt.shape[-1]).sum(axis=0)

    kt = jnp.concatenate([p[...] for p in pages], axis=-1)
    c0 = pl.multiple_of(g * (pg * ps), pg * ps)
    it_ref[:, pl.ds(c0, pg * ps)] = scores(kt)

    @pl.when(g == pl.num_programs(1) - 1)
    def _():
        tot = past + LANES
        sn = scores(kinew_ref[...].astype(F32))
        tq_i = lax.broadcasted_iota(I32, (t_new, LANES), 0)
        tk_i = lax.broadcasted_iota(I32, (t_new, LANES), 1)
        it_ref[:, past:tot] = jnp.where(tk_i <= tq_i, sn, -jnp.inf)
        x = it_ref[...]
        key_i = lax.broadcasted_iota(I32, (t_new, tot), 1)

        def count(hit):
            return jnp.sum(jnp.where(hit, 1.0, 0.0), axis=1, keepdims=True)

        def bis(_, lohi):
            lo, hi = lohi
            mid = _mid(lo, hi)
            ok = count(x >= _key_to_float(mid)) >= n_sel
            return jnp.where(ok, mid, lo), jnp.where(ok, hi, mid)

        lo, _ = lax.fori_loop(0, N_BISECT, bis,
                              (jnp.full((t_new, 1), KEY_LO, I32), jnp.full((t_new, 1), KEY_HI, I32)))
        tau = _key_to_float(lo)
        need = n_sel - count(x > tau)

        def tie(_, lohi):
            plo, phi = lohi
            pm = (plo + phi) >> 1
            ok = count((x == tau) & (key_i <= pm)) >= need
            return jnp.where(ok, plo, pm), jnp.where(ok, pm, phi)

        n_it = int(math.ceil(math.log2(tot))) + 1
        _, pcut = lax.fori_loop(0, n_it, tie,
                                (jnp.full((t_new, 1), -1, I32), jnp.full((t_new, 1), tot - 1, I32)))
        sel = (x > tau) | ((x == tau) & (key_i <= pcut))
        madd_ref[...] = jnp.where(sel, 0.0, NEG)


def _sample_select(page_table, qi_s, w_s, kinew, kidxT, pg, n_sel):
    db, n_pages = page_table.shape
    ps = kidxT.shape[-1]
    t_new = qi_s.shape[1] // N_IDX_HEADS
    tot = n_pages * ps + LANES
    assert n_pages % pg == 0

    def page_spec(u):
        return pl.BlockSpec((None, IDX_DIM, ps), lambda b, g, pt: (pt[b, g * pg + u], 0, 0))

    grid_spec = pltpu.PrefetchScalarGridSpec(
        num_scalar_prefetch=1,
        grid=(db, n_pages // pg),
        in_specs=[pl.BlockSpec((None,) + qi_s.shape[1:], lambda b, g, pt: (b, 0, 0)),
                  pl.BlockSpec((None,) + w_s.shape[1:], lambda b, g, pt: (b, 0, 0)),
                  pl.BlockSpec((None,) + kinew.shape[1:], lambda b, g, pt: (b, 0, 0))]
                 + [page_spec(u) for u in range(pg)],
        out_specs=pl.BlockSpec((None, t_new, tot), lambda b, g, pt: (b, 0, 0)),
        scratch_shapes=[pltpu.VMEM((t_new, tot), F32)])
    return pl.pallas_call(
        functools.partial(_sidx_body, pg=pg, n_pages=n_pages, t_new=t_new, n_sel=n_sel),
        grid_spec=grid_spec,
        out_shape=jax.ShapeDtypeStruct((db, t_new, tot), F32),
        compiler_params=pltpu.CompilerParams(dimension_semantics=("arbitrary", "arbitrary"),
                                             vmem_limit_bytes=VMEM_LIMIT),
        name="sample_select",
    )(page_table, qi_s, w_s, kinew, *([kidxT] * pg))


def _sattn_body(pt_ref, q_ref, madd_ref, maddn_ref, knew_ref, vnew_ref, bias_ref, *rest, pg):
    kpages = rest[:pg]
    vpages = rest[pg:2 * pg]
    o_ref = rest[2 * pg]
    m_ref, l_ref, acc_ref = rest[2 * pg + 1:]
    g = pl.program_id(1)
    last = pl.num_programs(1) - 1
    q = q_ref[...]
    ps = kpages[0].shape[-1]

    @pl.when(g == 0)
    def _():
        m_ref[...] = jnp.full(m_ref.shape, NEG, F32)
        l_ref[...] = jnp.zeros(l_ref.shape, F32)
        acc_ref[...] = jnp.zeros(acc_ref.shape, F32)

    def flash(s, vts):
        m_old = m_ref[...]
        m_new = jnp.maximum(m_old, jnp.max(s, axis=-1, keepdims=True))
        p = jnp.exp(s - m_new)
        alpha = jnp.exp(m_old - m_new)
        l_ref[...] = alpha * l_ref[...] + jnp.sum(p, axis=-1, keepdims=True)
        pb = p.astype(BF16)
        pv = jnp.zeros(acc_ref.shape, F32)
        for u, vt in enumerate(vts):
            pv = pv + jnp.einsum('htk,hdk->htd', pb[:, :, u * ps:(u + 1) * ps], vt,
                                 preferred_element_type=F32)
        acc_ref[...] = alpha * acc_ref[...] + pv
        m_ref[...] = m_new

    is_last = jnp.where(g == last, 1.0, 0.0)
    parts = []
    for u in range(pg):
        s = jnp.einsum('htd,hdk->htk', q, kpages[u][...].astype(BF16),
                       preferred_element_type=F32)
        if u == pg - 1:
            s = s + is_last * bias_ref[0]
        parts.append(s)
    s = jnp.concatenate(parts, axis=-1) + madd_ref[...][None]
    flash(s, [vp[...].astype(BF16) for vp in vpages])

    @pl.when(g == last)
    def _():
        sn = jnp.einsum('htd,hdk->htk', q, knew_ref[...], preferred_element_type=F32)
        sn = sn + bias_ref[1] + maddn_ref[...][None]
        flash(sn, [vnew_ref[...]])
        o_ref[...] = acc_ref[...] / l_ref[...]


def _sample_attention(page_table, q_s, madd, knew, vnew, bias_s, cache_kT, cache_vT, pg):
    db, n_pages = page_table.shape
    ps = cache_kT.shape[-1]
    t_new = q_s.shape[2]
    assert n_pages % pg == 0

    def page_spec(u):
        return pl.BlockSpec((None, N_HEADS, HEAD_DIM, ps),
                            lambda b, g, pt: (pt[b, g * pg + u], 0, 0, 0))

    per_b = lambda b, g, pt: (b, 0, 0, 0)
    grid_spec = pltpu.PrefetchScalarGridSpec(
        num_scalar_prefetch=1,
        grid=(db, n_pages // pg),
        in_specs=[pl.BlockSpec((None,) + q_s.shape[1:], per_b),
                  pl.BlockSpec((None, t_new, pg * ps), lambda b, g, pt: (b, 0, g)),
                  pl.BlockSpec((None, t_new, LANES), lambda b, g, pt: (b, 0, n_pages * ps // LANES)),
                  pl.BlockSpec((None,) + knew.shape[1:], per_b),
                  pl.BlockSpec((None,) + vnew.shape[1:], per_b),
                  pl.BlockSpec(bias_s.shape, lambda b, g, pt: (0, 0, 0, 0))]
                 + [page_spec(u) for u in range(pg)] * 2,
        out_specs=pl.BlockSpec((None, N_HEADS, t_new, HEAD_DIM), per_b),
        scratch_shapes=[pltpu.VMEM((N_HEADS, t_new, 1), F32),
                        pltpu.VMEM((N_HEADS, t_new, 1), F32),
                        pltpu.VMEM((N_HEADS, t_new, HEAD_DIM), F32)])
    return pl.pallas_call(
        functools.partial(_sattn_body, pg=pg),
        grid_spec=grid_spec,
        out_shape=jax.ShapeDtypeStruct((db, N_HEADS, t_new, HEAD_DIM), F32),
        compiler_params=pltpu.CompilerParams(dimension_semantics=("arbitrary", "arbitrary"),
                                             vmem_limit_bytes=VMEM_LIMIT),
        name="sample_attention",
    )(page_table, q_s, madd, madd, knew, vnew, bias_s, *([cache_kT] * pg), *([cache_vT] * pg))


def _outmlp_body(x_ref, a_ref, bg_ref, u_ref, prev_ref, cw_ref, wo_ref, gm_ref, wu_ref, wd_ref,
                 gf_ref, y_ref, *, seq_len, ff_chunk):
    tm = x_ref.shape[0]
    u = u_ref[...]
    w0 = cw_ref[0:1, :]
    w1 = cw_ref[1:2, :]
    w2 = cw_ref[2:3, :]
    if seq_len >= tm:
        first = (pl.program_id(0) % (seq_len // tm)) == 0
        halo = prev_ref[...] * jnp.where(first, 0.0, 1.0)
        row = lax.broadcasted_iota(I32, u.shape, 0)
        um1 = jnp.where(row == 0, halo[7:8, :], pltpu.roll(u, 1, 0))
        um2 = jnp.where(row == 0, halo[6:7, :],
                        jnp.where(row == 1, halo[7:8, :], pltpu.roll(u, 2, 0)))
    else:
        nseq = tm // seq_len
        u3 = u.reshape(nseq, seq_len, u.shape[-1])
        up = jnp.concatenate([prev_ref[...], u3], axis=1)
        um1 = up[:, 1:1 + seq_len].reshape(u.shape)
        um2 = up[:, 0:seq_len].reshape(u.shape)
    y = um2 * w0
    y = y + um1 * w1
    y = y + u * w2
    b = (bg_ref[...] * y).astype(BF16)
    ab = jnp.concatenate([a_ref[...], b], axis=-1)
    x1 = x_ref[...] + jnp.dot(ab, wo_ref[...], preferred_element_type=F32)
    ms = jnp.mean(x1 * x1, axis=-1, keepdims=True)
    hn = ((x1 * lax.rsqrt(ms + EPS)) * gm_ref[...]).astype(BF16)
    acc = jnp.zeros(x1.shape, F32)
    d_ff = wu_ref.shape[1]
    for c in range(d_ff // ff_chunk):
        sl = slice(c * ff_chunk, (c + 1) * ff_chunk)
        up_c = jnp.dot(hn, wu_ref[:, sl], preferred_element_type=F32)
        r = jnp.maximum(up_c, 0.0)
        acc = acc + jnp.dot((r * r).astype(BF16), wd_ref[sl, :], preferred_element_type=F32)
    x2 = x1 + acc
    ms2 = jnp.mean(x2 * x2, axis=-1, keepdims=True)
    y_ref[...] = (x2 * lax.rsqrt(ms2 + EPS)) * gf_ref[...]


def _out_mlp(x2d, a, bg, u, prev, conv_w, wo, g_mlp, wu, wd, g_final, tm, seq_len):
    n, d = x2d.shape
    dc = bg.shape[1]
    row = lambda i: (i, 0)
    c2 = lambda i: (0, 0)
    if seq_len >= tm:
        prev_spec = pl.BlockSpec((8, dc), lambda i: (jnp.maximum(i * (tm // 8) - 1, 0), 0))
        prev_arg = u
    else:
        nseq = tm // seq_len
        prev_spec = pl.BlockSpec((nseq,) + prev.shape[1:], lambda i: (i, 0, 0))
        prev_arg = prev
    single = dict(pipeline_mode=pl.Buffered(1))
    return pl.pallas_call(
        functools.partial(_outmlp_body, seq_len=seq_len, ff_chunk=1024),
        grid=(n // tm,),
        in_specs=[pl.BlockSpec((tm, d), row),
                  pl.BlockSpec((tm, a.shape[1]), row),
                  pl.BlockSpec((tm, dc), row),
                  pl.BlockSpec((tm, dc), row),
                  prev_spec,
                  pl.BlockSpec(conv_w.shape, c2),
                  pl.BlockSpec(wo.shape, c2, **single),
                  pl.BlockSpec((1, d), c2),
                  pl.BlockSpec(wu.shape, c2, **single),
                  pl.BlockSpec(wd.shape, c2, **single),
                  pl.BlockSpec((1, d), c2)],
        out_specs=pl.BlockSpec((tm, d), row),
        out_shape=jax.ShapeDtypeStruct((n, d), F32),
        compiler_params=pltpu.CompilerParams(dimension_semantics=("arbitrary",),
                                             vmem_limit_bytes=VMEM_LIMIT),
        name="out_mlp",
    )(x2d, a, bg, u, prev_arg, conv_w, wo, g_mlp, wu, wd, g_final)


def _pick_tile(n, pref):
    t = min(pref, n)
    while n % t:
        t //= 2
    return t


def kernel(x_prompt, x_sample, cache_k, cache_v, cache_kidx, state_conv, page_table, rel_bias,
           g_mix, w_in, conv_w, w_out, g_mlp, w_up, w_down, g_final):
    depth = w_in.shape[0]
    assert depth == 1, "single-layer step"
    batch, seq, d_model = x_prompt.shape
    db, t_new, _ = x_sample.shape
    n_pages = page_table.shape[1]
    ps = cache_k.shape[2]
    past = n_pages * ps
    assert ps == LANES and t_new == 8

    tq = _pick_tile(seq, 256)
    n_sel_p = min(TOPK_MAX, seq // 4)
    n_sel_s = min(TOPK_MAX, (past + t_new) // 4)
    assert _far_bucket_is_constant(tq + 1, seq) and _far_bucket_is_constant(ps + 1, past + t_new)

    w = w_in[0]
    cq, ck_, cv, cqi, cki, cwi, cbg, ccg, ch = np.cumsum(
        [0, D_ATTN, D_ATTN, D_ATTN, N_IDX_HEADS * IDX_DIM, IDX_DIM, N_IDX_HEADS, D_ATTN, D_ATTN])
    end = ch + D_ATTN
    wt = jnp.pad(w[:, cq:cbg].T, ((0, T_END - cbg), (0, 0))).astype(BF16)
    wn = jnp.concatenate(
        [w[:, ck_:cv], w[:, cbg:end], jnp.pad(w[:, cki:cwi], ((0, 0), (0, LANES - IDX_DIM)))],
        axis=1).astype(BF16)
    wo = w_out[0].astype(BF16)
    wu = w_up[0].astype(BF16)
    wd = w_down[0].astype(BF16)
    gmix = g_mix[0][None]
    gmlp = g_mlp[0][None]
    gfin = g_final[None]
    cw = conv_w[0]

    xp2 = x_prompt.reshape(batch * seq, d_model)
    tm_p = _pick_tile(batch * seq, 512)
    (qT, kT, vT, vTb, qiT, kiT, wiT, kn, kin, bg, u) = _in_proj(xp2, gmix, wn, wt, tm_p)
    bias_p = _bias_tables(rel_bias, tq, tq, (0, tq), -1, 1)
    a_p = _prompt_attention(qT, qiT, wiT, kn, kin, vTb, bias_p, batch, seq, tq, n_sel_p)
    y_prompt = _out_mlp(xp2, a_p, bg, u, None, cw, wo, gmlp, wu, wd, gfin,
                        _pick_tile(seq, 512), seq).reshape(batch, seq, d_model)

    def heads_out(t, b_, s_):
        return t.reshape(N_HEADS, HEAD_DIM, b_, s_).transpose(2, 3, 0, 1)[None]

    k_prompt = heads_out(kT, batch, seq)
    v_prompt = heads_out(vT, batch, seq)
    kidx_prompt = kiT.reshape(IDX_DIM, batch, seq).transpose(1, 2, 0)[None]
    conv_prompt = u.reshape(batch, seq, D_ATTN)[:, seq - (CONV_WIDTH - 1):][None]

    ns = db * t_new
    xs2 = x_sample.reshape(ns, d_model)
    (qTs, kTs, vTs, vTbs, qiTs, kiTs, wiTs, _, _, bgs, us) = _in_proj(xs2, gmix, wn, wt, ns)
    qi_s = qiTs.reshape(N_IDX_HEADS, IDX_DIM, db, t_new).transpose(2, 0, 3, 1).reshape(
        db, N_IDX_HEADS * t_new, IDX_DIM)
    w_s = wiTs.reshape(N_IDX_HEADS, db, t_new).transpose(1, 0, 2).reshape(db, N_IDX_HEADS * t_new, 1)
    kinew = jnp.pad(kiTs.reshape(IDX_DIM, db, t_new).transpose(1, 0, 2),
                    ((0, 0), (0, 0), (0, LANES - t_new))).astype(BF16)
    q_s = qTs.reshape(N_HEADS, HEAD_DIM, db, t_new).transpose(2, 0, 3, 1)
    pad_new = lambda t: jnp.pad(t.reshape(N_HEADS, HEAD_DIM, db, t_new).transpose(2, 0, 1, 3),
                                ((0, 0), (0, 0), (0, 0), (0, LANES - t_new))).astype(BF16)
    knew = pad_new(kTs)
    vnew = pad_new(vTs)
    kidxT = cache_kidx[0].transpose(0, 2, 1)
    cache_kT = cache_k[0].transpose(0, 2, 3, 1)
    cache_vT = cache_v[0].transpose(0, 2, 3, 1)

    madd = _sample_select(page_table, qi_s, w_s, kinew, kidxT, _pick_tile(n_pages, 16), n_sel_s)
    bias_s = _bias_tables(rel_bias, t_new, LANES, (ps, 0), 1, -1)
    o_s = _sample_attention(page_table, q_s, madd, knew, vnew, bias_s, cache_kT, cache_vT,
                            _pick_tile(n_pages, 8))
    a_s = o_s.transpose(0, 2, 1, 3).reshape(ns, D_ATTN).astype(BF16)
    y_sample = _out_mlp(xs2, a_s, bgs, us, state_conv[0], cw, wo, gmlp, wu, wd, gfin,
                        ns, t_new).reshape(db, t_new, d_model)

    k_sample = heads_out(kTs, db, t_new)
    v_sample = heads_out(vTs, db, t_new)
    kidx_sample = kiTs.reshape(IDX_DIM, db, t_new).transpose(1, 2, 0)[None]
    conv_sample = us.reshape(db, t_new, D_ATTN)[:, t_new - (CONV_WIDTH - 1):][None]

    return (y_prompt, y_sample, k_prompt, v_prompt, kidx_prompt, conv_prompt,
            k_sample, v_sample, kidx_sample, conv_sample)
```

```python
import functools
import math

import jax
import jax.numpy as jnp
import numpy as np
from jax import lax
from jax.experimental import pallas as pl
from jax.experimental.pallas import tpu as pltpu

F32 = jnp.float32
BF16 = jnp.bfloat16
I32 = jnp.int32

HEAD_DIM = 64
N_HEADS = 8
N_IDX_HEADS = 8
IDX_DIM = 32
D_ATTN = N_HEADS * HEAD_DIM
TOPK_MAX = 256
CONV_WIDTH = 3
N_BUCKETS = 32
MAX_DISTANCE = 128
EPS = 1e-6
ATTN_SCALE = HEAD_DIM ** -0.5
INDEX_SCALE = (IDX_DIM ** -0.5) * (N_IDX_HEADS ** -0.5)
LOG2E = math.log2(math.e)

LANES = 128
SUBLANES = 8
NEG = -1e30
VMEM_LIMIT = 56 * 1024 * 1024


def _float_key(v):
    b = int(np.array(v, np.float32).view(np.int32))
    return b if b >= 0 else b ^ 0x7FFFFFFF


KEY_LO = _float_key(-np.finfo(np.float32).max)
KEY_HI = _float_key(np.inf)
N_BISECT = 33
LOOKAHEAD = 2
N_SBUF = 4


def _key_to_float(k):
    bits = k ^ ((k >> 31) & 0x7FFFFFFF)
    return lax.bitcast_convert_type(bits, F32)


def _mid(lo, hi):
    return (lo >> 1) + (hi >> 1) + (lo & hi & 1)


def _tree_sum(parts):
    parts = list(parts)
    while len(parts) > 1:
        nxt = [parts[k] + parts[k + 1] for k in range(0, len(parts) - 1, 2)]
        if len(parts) % 2:
            nxt.append(parts[-1])
        parts = nxt
    return parts[0]


def _bias_body(rb_ref, o_ref, *, offs, sa, sb):
    w = pl.program_id(0)
    h = pl.program_id(1)
    shape = o_ref.shape
    a = lax.broadcasted_iota(I32, shape, 0)
    b = lax.broadcasted_iota(I32, shape, 1)
    off = jnp.where(w == 0, offs[0], offs[1])
    dist = off + sa * a + sb * b
    n = jnp.maximum(dist, 0)
    max_exact = N_BUCKETS // 2
    nf = jnp.maximum(n, 1).astype(F32)
    large = max_exact + (jnp.log(nf / max_exact) / math.log(MAX_DISTANCE / max_exact)
                         * (N_BUCKETS - max_exact)).astype(I32)
    large = jnp.minimum(large, N_BUCKETS - 1)
    bucket = jnp.where(n < max_exact, n, large)
    val = jnp.zeros(shape, F32)
    for k in range(N_BUCKETS):
        val = jnp.where(bucket == k, rb_ref[k, h], val)
    o_ref[...] = (val - rb_ref[N_BUCKETS - 1, h]) * LOG2E


def _bias_tables(rel_bias, rows, cols, offs, sa, sb):
    return pl.pallas_call(
        functools.partial(_bias_body, offs=offs, sa=sa, sb=sb),
        grid=(2, N_HEADS),
        in_specs=[pl.BlockSpec(memory_space=pltpu.SMEM)],
        out_specs=pl.BlockSpec((None, None, rows, cols), lambda w, h: (w, h, 0, 0)),
        out_shape=jax.ShapeDtypeStruct((2, N_HEADS, rows, cols), F32),
        name="bias_tables",
    )(rel_bias)


def _far_bucket_is_constant(min_dist, max_dist):
    d = np.arange(min_dist, max_dist + 1, dtype=np.float64)
    me = N_BUCKETS // 2
    b = me + np.floor(np.log(d / me) / math.log(MAX_DISTANCE / me) * (N_BUCKETS - me))
    return bool(np.all(np.minimum(b, N_BUCKETS - 1) == N_BUCKETS - 1)) and min_dist >= me


T_Q, T_K, T_V, T_QI, T_KI, T_WI, T_END = 0, 512, 1024, 1536, 1792, 1824, 1840
N_K, N_BG, N_CG, N_H, N_KI, N_END = 0, 512, 1024, 1536, 2048, 2176


def _inproj_body(x_ref, g_ref, wn_ref, wt_ref,
                 qT_ref, kT_ref, vT_ref, vTb_ref, qiT_ref, kiT_ref, wiT_ref,
                 kn_ref, kin_ref, bg_ref, u_ref):
    x = x_ref[...]
    ms = jnp.mean(x * x, axis=-1, keepdims=True)
    xn = ((x * lax.rsqrt(ms + EPS)) * g_ref[...]).astype(BF16)

    def nat(a, b):
        return jnp.dot(xn, wn_ref[:, a:b], preferred_element_type=F32)

    def tra(a, b):
        return lax.dot_general(wt_ref[a:b, :], xn, (((1,), (1,)), ((), ())),
                               preferred_element_type=F32)

    kn_ref[...] = nat(N_K, N_BG).astype(BF16)
    bg_ref[...] = nat(N_BG, N_CG)
    u_ref[...] = nat(N_CG, N_H) * nat(N_H, N_KI)
    kin_ref[...] = nat(N_KI, N_END).astype(BF16)

    qT_ref[...] = (tra(T_Q, T_K) * (ATTN_SCALE * LOG2E)).astype(BF16)
    kT_ref[...] = tra(T_K, T_V)
    vt = tra(T_V, T_QI)
    vT_ref[...] = vt
    vTb_ref[...] = vt.astype(BF16)
    qiT_ref[...] = tra(T_QI, T_KI).astype(BF16)
    kiT_ref[...] = tra(T_KI, T_WI)
    wiT_ref[...] = tra(T_WI, T_END)[0:N_IDX_HEADS, :] * INDEX_SCALE


def _in_proj(x3, g, wn, wt, tm):
    nb, seq, d = x3.shape
    assert seq % tm == 0
    const = lambda b, i: (0, 0)
    row = lambda b, i: (b, i, 0)
    col = lambda b, i: (b, 0, i)
    di = N_IDX_HEADS * IDX_DIM
    outs = [
        ((nb, D_ATTN, seq), BF16, (None, D_ATTN, tm), col),
        ((nb, D_ATTN, seq), F32, (None, D_ATTN, tm), col),
        ((nb, D_ATTN, seq), F32, (None, D_ATTN, tm), col),
        ((nb, D_ATTN, seq), BF16, (None, D_ATTN, tm), col),
        ((nb, di, seq), BF16, (None, di, tm), col),
        ((nb, IDX_DIM, seq), F32, (None, IDX_DIM, tm), col),
        ((nb, N_IDX_HEADS, seq), F32, (None, N_IDX_HEADS, tm), col),
        ((nb, seq, D_ATTN), BF16, (None, tm, D_ATTN), row),
        ((nb, seq, LANES), BF16, (None, tm, LANES), row),
        ((nb, seq, D_ATTN), F32, (None, tm, D_ATTN), row),
        ((nb, seq, D_ATTN), F32, (None, tm, D_ATTN), row),
    ]
    return pl.pallas_call(
        _inproj_body,
        grid=(nb, seq // tm),
        in_specs=[pl.BlockSpec((None, tm, d), row),
                  pl.BlockSpec((1, d), const),
                  pl.BlockSpec(wn.shape, const),
                  pl.BlockSpec(wt.shape, const)],
        out_specs=[pl.BlockSpec(bs, im) for (_, _, bs, im) in outs],
        out_shape=[jax.ShapeDtypeStruct(s, dt) for (s, dt, _, _) in outs],
        compiler_params=pltpu.CompilerParams(dimension_semantics=("arbitrary", "arbitrary"),
                                             vmem_limit_bytes=VMEM_LIMIT),
        name="in_proj",
    )(x3, g, wn, wt)


def _pattn_body(qT_ref, qiT_ref, wiT_ref, kn_ref, kin_ref, vT_ref, bias_ref, a_ref,
                it_ref, qz_ref, m_ref, l_ref, acc_ref, p_ref, madd_ref, s_ref, *, tq, n_sel):
    i = pl.program_id(1)
    nch = i + 1
    ck = tq
    s_tot = kn_ref.shape[0]
    q_idx = i * tq + lax.broadcasted_iota(I32, (ck, tq), 1)
    row_iota = lax.broadcasted_iota(I32, (ck, tq), 0)

    def idx_chunk(j, carry):
        r0 = pl.multiple_of(j * ck, ck)
        ki = kin_ref[pl.ds(r0, ck), :][:, 0:IDX_DIM]
        terms = []
        for h in range(N_IDX_HEADS):
            s = jnp.dot(ki, qiT_ref[h * IDX_DIM:(h + 1) * IDX_DIM, :],
                        preferred_element_type=F32)
            terms.append(wiT_ref[h:h + 1, :] * jnp.maximum(s, 0.0))
        acc = _tree_sum(terms)
        acc = jnp.where(r0 + row_iota <= q_idx, acc, -jnp.inf)
        it_ref[pl.ds(r0, ck), :] = acc
        return carry

    lax.fori_loop(0, nch, idx_chunk, 0)

    def count(pred):
        def body(j, c8):
            r0 = pl.multiple_of(j * ck, ck)
            x = it_ref[pl.ds(r0, ck), :]
            hit = jnp.where(pred(x, r0), 1.0, 0.0)
            return c8 + _tree_sum(hit[r * SUBLANES:(r + 1) * SUBLANES]
                                  for r in range(ck // SUBLANES))
        c8 = lax.fori_loop(0, nch, body, jnp.zeros((SUBLANES, tq), F32))
        return c8.sum(axis=0, keepdims=True)

    def bis(_, st):
        lo, hi, c_lo, c_hi = st
        mid = _mid(lo, hi)
        midf = _key_to_float(mid)
        c = count(lambda x, r0: x >= midf)
        ok = c >= n_sel
        return (jnp.where(ok, mid, lo), jnp.where(ok, hi, mid),
                jnp.where(ok, c, c_lo), jnp.where(ok, c_hi, c))

    zero = jnp.zeros((1, tq), F32)
    lo, _, c_ge, c_gt = lax.fori_loop(
        0, N_BISECT, bis,
        (jnp.full((1, tq), KEY_LO, I32), jnp.full((1, tq), KEY_HI, I32), zero, zero))
    tau = _key_to_float(lo)
    need = n_sel - c_gt

    p_ref[...] = jnp.full((1, tq), s_tot, I32)

    @pl.when(jnp.max(c_ge) > n_sel)
    def _():
        def tie(_, lohi):
            plo, phi = lohi
            pm = (plo + phi) >> 1
            ok = count(lambda x, r0: (x == tau) & (r0 + row_iota <= pm)) >= need
            return jnp.where(ok, plo, pm), jnp.where(ok, pm, phi)
        n_it = int(math.ceil(math.log2(s_tot))) + 1
        _, phi = lax.fori_loop(0, n_it, tie,
                               (jnp.full((1, tq), -1, I32), jnp.full((1, tq), s_tot - 1, I32)))
        p_ref[...] = phi

    pcut = p_ref[...]

    half = lax.broadcasted_iota(I32, (2 * HEAD_DIM, tq), 0) // HEAD_DIM
    for h in range(N_HEADS):
        pair = qT_ref[(h // 2) * 2 * HEAD_DIM:(h // 2 + 1) * 2 * HEAD_DIM, :]
        qz_ref[h] = jnp.where(half == (h % 2), pair, jnp.zeros_like(pair))
    m_ref[...] = jnp.full(m_ref.shape, NEG, F32)
    l_ref[...] = jnp.zeros(l_ref.shape, F32)
    acc_ref[...] = jnp.zeros(acc_ref.shape, F32)

    def attend(j, bias_w):
        r0 = pl.multiple_of(j * ck, ck)
        x = it_ref[pl.ds(r0, ck), :]
        sel = (x > tau) | ((x == tau) & (r0 + row_iota <= pcut))
        madd_ref[...] = jnp.where(sel, 0.0, NEG)

        def qk(h):
            kp = kn_ref[pl.ds(r0, ck), (h // 2) * 2 * HEAD_DIM:(h // 2 + 1) * 2 * HEAD_DIM]
            return jnp.dot(kp, qz_ref[h], preferred_element_type=F32)

        ones = jnp.ones((2 * SUBLANES, ck), BF16)
        for h in range(min(LOOKAHEAD, N_HEADS)):
            s_ref[h % N_SBUF] = qk(h)
        for h in range(N_HEADS):
            if h + LOOKAHEAD < N_HEADS:
                s_ref[(h + LOOKAHEAD) % N_SBUF] = qk(h + LOOKAHEAD)
            s = s_ref[h % N_SBUF] + madd_ref[...]
            if bias_w is not None:
                s = s + bias_ref[bias_w, h]
            m_old = m_ref[h:h + 1, :]
            m_new = jnp.maximum(m_old, jnp.max(s, axis=0, keepdims=True))
            p = jnp.exp2(s - m_new)
            alpha = jnp.exp2(m_old - m_new)
            vt = vT_ref[h * HEAD_DIM:(h + 1) * HEAD_DIM, pl.ds(r0, ck)]
            pv = jnp.dot(jnp.concatenate([vt, ones], axis=0), p.astype(BF16),
                         preferred_element_type=F32)
            l_ref[h:h + 1, :] = alpha * l_ref[h:h + 1, :] + pv[HEAD_DIM:HEAD_DIM + 1, :]
            acc_ref[h * HEAD_DIM:(h + 1) * HEAD_DIM, :] = (
                alpha * acc_ref[h * HEAD_DIM:(h + 1) * HEAD_DIM, :] + pv[0:HEAD_DIM, :])
            m_ref[h:h + 1, :] = m_new

    def far_chunk(j, carry):
        attend(j, None)
        return carry

    lax.fori_loop(0, jnp.maximum(i - 1, 0), far_chunk, 0)

    @pl.when(i >= 1)
    def _():
        attend(i - 1, 1)

    attend(i, 0)

    for h in range(N_HEADS):
        sl = slice(h * HEAD_DIM, (h + 1) * HEAD_DIM)
        acc_ref[sl, :] = acc_ref[sl, :] / l_ref[h:h + 1, :]
    a_ref[...] = acc_ref[...].T.astype(BF16)


def _prompt_attention(qT, qiT, wiT, kn, kin, vTb, bias, tq, n_sel):
    batch, seq, _ = kn.shape
    nq = seq // tq
    blk = lambda b, i: (b, 0, i)
    per_b = lambda b, i: (b, 0, 0)
    return pl.pallas_call(
        functools.partial(_pattn_body, tq=tq, n_sel=n_sel),
        grid=(batch, nq),
        in_specs=[pl.BlockSpec((None, D_ATTN, tq), blk),
                  pl.BlockSpec((None, N_IDX_HEADS * IDX_DIM, tq), blk),
                  pl.BlockSpec((None, N_IDX_HEADS, tq), blk),
                  pl.BlockSpec((None, seq, D_ATTN), per_b),
                  pl.BlockSpec((None, seq, LANES), per_b),
                  pl.BlockSpec((None, D_ATTN, seq), per_b),
                  pl.BlockSpec(bias.shape, lambda b, i: (0, 0, 0, 0))],
        out_specs=pl.BlockSpec((None, tq, D_ATTN), lambda b, i: (b, i, 0)),
        out_shape=jax.ShapeDtypeStruct((batch, seq, D_ATTN), BF16),
        scratch_shapes=[pltpu.VMEM((seq, tq), F32),
                        pltpu.VMEM((N_HEADS, 2 * HEAD_DIM, tq), BF16),
                        pltpu.VMEM((N_HEADS, tq), F32),
                        pltpu.VMEM((N_HEADS, tq), F32),
                        pltpu.VMEM((D_ATTN, tq), F32),
                        pltpu.VMEM((1, tq), I32),
                        pltpu.VMEM((tq, tq), F32),
                        pltpu.VMEM((N_SBUF, tq, tq), F32)],
        compiler_params=pltpu.CompilerParams(dimension_semantics=("arbitrary", "arbitrary"),
                                             vmem_limit_bytes=VMEM_LIMIT),
        name="prompt_attention",
    )(qT, qiT, wiT, kn, kin, vTb, bias)


def _sidx_body(pt_ref, qi_ref, w_ref, kinew_ref, *rest, pg, n_pages, t_new, n_sel):
    pages = rest[:pg]
    madd_ref = rest[pg]
    it_ref, p_ref = rest[pg + 1:]
    g = pl.program_id(1)
    ps = pages[0].shape[-1]
    past = n_pages * ps
    tot = past + LANES
    n_slab = tot // LANES
    qi = qi_ref[...]
    w = w_ref[...]

    def scores(kt):
        s = jnp.dot(qi, kt.astype(BF16), preferred_element_type=F32)
        r = jnp.maximum(s, 0.0) * w
        return _tree_sum(r[h * t_new:(h + 1) * t_new] for h in range(N_IDX_HEADS))

    kt = jnp.concatenate([p[...] for p in pages], axis=-1)
    c0 = pl.multiple_of(g * (pg * ps), pg * ps)
    it_ref[:, pl.ds(c0, pg * ps)] = scores(kt)

    @pl.when(g == pl.num_programs(1) - 1)
    def _():
        sn = scores(kinew_ref[...].astype(F32))
        tq_i = lax.broadcasted_iota(I32, (t_new, LANES), 0)
        lane_i = lax.broadcasted_iota(I32, (t_new, LANES), 1)
        it_ref[:, past:tot] = jnp.where(lane_i <= tq_i, sn, -jnp.inf)

        def count(pred):
            parts = []
            for sl in range(n_slab):
                x = it_ref[:, sl * LANES:(sl + 1) * LANES]
                parts.append(jnp.where(pred(x, sl * LANES), 1.0, 0.0))
            return jnp.sum(_tree_sum(parts), axis=1, keepdims=True)

        def bis(_, st):
            lo, hi, c_lo, c_hi = st
            mid = _mid(lo, hi)
            midf = _key_to_float(mid)
            c = count(lambda x, c0_: x >= midf)
            ok = c >= n_sel
            return (jnp.where(ok, mid, lo), jnp.where(ok, hi, mid),
                    jnp.where(ok, c, c_lo), jnp.where(ok, c_hi, c))

        zero = jnp.zeros((t_new, 1), F32)
        lo, _, c_ge, c_gt = lax.fori_loop(
            0, N_BISECT, bis,
            (jnp.full((t_new, 1), KEY_LO, I32), jnp.full((t_new, 1), KEY_HI, I32), zero, zero))
        tau = _key_to_float(lo)
        need = n_sel - c_gt
        p_ref[...] = jnp.full((t_new, 1), tot, I32)

        @pl.when(jnp.max(c_ge) > n_sel)
        def _():
            def tie(_, lohi):
                plo, phi = lohi
                pm = (plo + phi) >> 1
                ok = count(lambda x, c0_: (x == tau) & (c0_ + lane_i <= pm)) >= need
                return jnp.where(ok, plo, pm), jnp.where(ok, pm, phi)
            n_it = int(math.ceil(math.log2(tot))) + 1
            _, phi = lax.fori_loop(0, n_it, tie, (jnp.full((t_new, 1), -1, I32),
                                                  jnp.full((t_new, 1), tot - 1, I32)))
            p_ref[...] = phi

        pcut = p_ref[...]
        for sl in range(n_slab):
            x = it_ref[:, sl * LANES:(sl + 1) * LANES]
            sel = (x > tau) | ((x == tau) & (sl * LANES + lane_i <= pcut))
            madd_ref[:, sl * LANES:(sl + 1) * LANES] = jnp.where(sel, 0.0, NEG)


def _sample_select(page_table, qi_s, w_s, kinew, kidxT, pg, n_sel):
    db, n_pages = page_table.shape
    ps = kidxT.shape[-1]
    t_new = qi_s.shape[1] // N_IDX_HEADS
    tot = n_pages * ps + LANES
    assert n_pages % pg == 0 and ps == LANES

    def page_spec(u):
        return pl.BlockSpec((None, IDX_DIM, ps), lambda b, g, pt: (pt[b, g * pg + u], 0, 0))

    grid_spec = pltpu.PrefetchScalarGridSpec(
        num_scalar_prefetch=1,
        grid=(db, n_pages // pg),
        in_specs=[pl.BlockSpec((None,) + qi_s.shape[1:], lambda b, g, pt: (b, 0, 0)),
                  pl.BlockSpec((None,) + w_s.shape[1:], lambda b, g, pt: (b, 0, 0)),
                  pl.BlockSpec((None,) + kinew.shape[1:], lambda b, g, pt: (b, 0, 0))]
                 + [page_spec(u) for u in range(pg)],
        out_specs=pl.BlockSpec((None, t_new, tot), lambda b, g, pt: (b, 0, 0)),
        scratch_shapes=[pltpu.VMEM((t_new, tot), F32), pltpu.VMEM((t_new, 1), I32)])
    return pl.pallas_call(
        functools.partial(_sidx_body, pg=pg, n_pages=n_pages, t_new=t_new, n_sel=n_sel),
        grid_spec=grid_spec,
        out_shape=jax.ShapeDtypeStruct((db, t_new, tot), F32),
        compiler_params=pltpu.CompilerParams(dimension_semantics=("arbitrary", "arbitrary"),
                                             vmem_limit_bytes=VMEM_LIMIT),
        name="sample_select",
    )(page_table, qi_s, w_s, kinew, *([kidxT] * pg))


def _sattn_body(pt_ref, qbd_ref, madd_ref, maddn_ref, knew_ref, vnew_ref, bias_ref, *rest, pg):
    kpages = rest[:pg]
    vpages = rest[pg:2 * pg]
    o_ref = rest[2 * pg]
    m_ref, l_ref, acc_ref = rest[2 * pg + 1:]
    g = pl.program_id(1)
    last = pl.num_programs(1) - 1
    qbd = qbd_ref[...]
    ps = kpages[0].shape[-1]
    t_new = madd_ref.shape[0]
    hd = N_HEADS * HEAD_DIM

    @pl.when(g == 0)
    def _():
        m_ref[...] = jnp.full(m_ref.shape, NEG, F32)
        l_ref[...] = jnp.zeros(l_ref.shape, F32)
        acc_ref[...] = jnp.zeros(acc_ref.shape, F32)

    def flash(s, vt):
        m_old = m_ref[...]
        m_new = jnp.maximum(m_old, jnp.max(s, axis=-1, keepdims=True))
        p = jnp.exp2(s - m_new)
        alpha = jnp.exp2(m_old - m_new)
        l_ref[...] = alpha * l_ref[...] + jnp.sum(p, axis=-1, keepdims=True)
        pv = lax.dot_general(p.astype(BF16), vt, (((1,), (1,)), ((), ())),
                             preferred_element_type=F32)
        acc_ref[...] = alpha * acc_ref[...] + pv
        m_ref[...] = m_new

    def add_rows(s, add):
        n = s.shape[-1]
        return (s.reshape(N_HEADS, t_new, n) + add[None]).reshape(N_HEADS * t_new, n)

    kt = jnp.concatenate([kp[...].reshape(hd, ps) for kp in kpages], axis=-1).astype(BF16)
    vt = jnp.concatenate([vp[...].reshape(hd, ps) for vp in vpages], axis=-1).astype(BF16)
    s = jnp.dot(qbd, kt, preferred_element_type=F32)
    s = add_rows(s, madd_ref[...])
    is_last = jnp.where(g == last, 1.0, 0.0)
    tail = s[:, (pg - 1) * ps:] + is_last * bias_ref[0].reshape(N_HEADS * t_new, ps)
    s = jnp.concatenate([s[:, :(pg - 1) * ps], tail], axis=-1)
    flash(s, vt)

    @pl.when(g == last)
    def _():
        sn = jnp.dot(qbd, knew_ref[...], preferred_element_type=F32)
        sn = add_rows(sn + bias_ref[1].reshape(N_HEADS * t_new, LANES), maddn_ref[...])
        flash(sn, vnew_ref[...])
        out = acc_ref[...] / l_ref[...]
        for h in range(N_HEADS):
            o_ref[h] = out[h * t_new:(h + 1) * t_new, h * HEAD_DIM:(h + 1) * HEAD_DIM]


def _sample_attention(page_table, qbd, madd, knew, vnew, bias_s, cache_kT, cache_vT, pg):
    db, n_pages = page_table.shape
    ps = cache_kT.shape[-1]
    t_new = madd.shape[1]
    assert n_pages % pg == 0

    def page_spec(u):
        return pl.BlockSpec((None, N_HEADS, HEAD_DIM, ps),
                            lambda b, g, pt: (pt[b, g * pg + u], 0, 0, 0))

    per_b = lambda b, g, pt: (b, 0, 0)
    grid_spec = pltpu.PrefetchScalarGridSpec(
        num_scalar_prefetch=1,
        grid=(db, n_pages // pg),
        in_specs=[pl.BlockSpec((None,) + qbd.shape[1:], per_b),
                  pl.BlockSpec((None, t_new, pg * ps), lambda b, g, pt: (b, 0, g)),
                  pl.BlockSpec((None, t_new, LANES), lambda b, g, pt: (b, 0, n_pages * ps // LANES)),
                  pl.BlockSpec((None,) + knew.shape[1:], per_b),
                  pl.BlockSpec((None,) + vnew.shape[1:], per_b),
                  pl.BlockSpec(bias_s.shape, lambda b, g, pt: (0, 0, 0, 0))]
                 + [page_spec(u) for u in range(pg)] * 2,
        out_specs=pl.BlockSpec((None, N_HEADS, t_new, HEAD_DIM), lambda b, g, pt: (b, 0, 0, 0)),
        scratch_shapes=[pltpu.VMEM((N_HEADS * t_new, 1), F32),
                        pltpu.VMEM((N_HEADS * t_new, 1), F32),
                        pltpu.VMEM((N_HEADS * t_new, N_HEADS * HEAD_DIM), F32)])
    return pl.pallas_call(
        functools.partial(_sattn_body, pg=pg),
        grid_spec=grid_spec,
        out_shape=jax.ShapeDtypeStruct((db, N_HEADS, t_new, HEAD_DIM), F32),
        compiler_params=pltpu.CompilerParams(dimension_semantics=("arbitrary", "arbitrary"),
                                             vmem_limit_bytes=VMEM_LIMIT),
        name="sample_attention",
    )(page_table, qbd, madd, madd, knew, vnew, bias_s, *([cache_kT] * pg), *([cache_vT] * pg))


def _outmlp_body(x_ref, a_ref, bg_ref, u_ref, prev_ref, cw_ref, wo_ref, gm_ref, wu_ref, wd_ref,
                 gf_ref, y_ref, *, seq_len, ff_chunk):
    tm = x_ref.shape[0]
    u = u_ref[...]
    w0 = cw_ref[0:1, :]
    w1 = cw_ref[1:2, :]
    w2 = cw_ref[2:3, :]
    if seq_len >= tm:
        first = (pl.program_id(0) % (seq_len // tm)) == 0
        halo = prev_ref[...] * jnp.where(first, 0.0, 1.0)
        row = lax.broadcasted_iota(I32, u.shape, 0)
        um1 = jnp.where(row == 0, halo[7:8, :], pltpu.roll(u, 1, 0))
        um2 = jnp.where(row == 0, halo[6:7, :],
                        jnp.where(row == 1, halo[7:8, :], pltpu.roll(u, 2, 0)))
    else:
        nseq = tm // seq_len
        u3 = u.reshape(nseq, seq_len, u.shape[-1])
        up = jnp.concatenate([prev_ref[...], u3], axis=1)
        um1 = up[:, 1:1 + seq_len].reshape(u.shape)
        um2 = up[:, 0:seq_len].reshape(u.shape)
    y = um2 * w0
    y = y + um1 * w1
    y = y + u * w2
    b = (bg_ref[...] * y).astype(BF16)
    ab = jnp.concatenate([a_ref[...], b], axis=-1)
    x1 = x_ref[...] + jnp.dot(ab, wo_ref[...], preferred_element_type=F32)
    ms = jnp.mean(x1 * x1, axis=-1, keepdims=True)
    hn = ((x1 * lax.rsqrt(ms + EPS)) * gm_ref[...]).astype(BF16)
    acc = jnp.zeros(x1.shape, F32)
    d_ff = wu_ref.shape[1]
    for c in range(d_ff // ff_chunk):
        sl = slice(c * ff_chunk, (c + 1) * ff_chunk)
        up_c = jnp.dot(hn, wu_ref[:, sl], preferred_element_type=F32)
        r = jnp.maximum(up_c, 0.0)
        acc = acc + jnp.dot((r * r).astype(BF16), wd_ref[sl, :], preferred_element_type=F32)
    x2 = x1 + acc
    ms2 = jnp.mean(x2 * x2, axis=-1, keepdims=True)
    y_ref[...] = (x2 * lax.rsqrt(ms2 + EPS)) * gf_ref[...]


def _out_mlp(x2d, a, bg, u, prev, conv_w, wo, g_mlp, wu, wd, g_final, tm, seq_len):
    n, d = x2d.shape
    dc = bg.shape[1]
    row = lambda i: (i, 0)
    c2 = lambda i: (0, 0)
    if seq_len >= tm:
        prev_spec = pl.BlockSpec((SUBLANES, dc),
                                 lambda i: (jnp.maximum(i * (tm // SUBLANES) - 1, 0), 0))
        prev_arg = u
    else:
        nseq = tm // seq_len
        prev_spec = pl.BlockSpec((nseq,) + prev.shape[1:], lambda i: (i, 0, 0))
        prev_arg = prev
    single = dict(pipeline_mode=pl.Buffered(1))
    return pl.pallas_call(
        functools.partial(_outmlp_body, seq_len=seq_len, ff_chunk=1024),
        grid=(n // tm,),
        in_specs=[pl.BlockSpec((tm, d), row),
                  pl.BlockSpec((tm, a.shape[1]), row),
                  pl.BlockSpec((tm, dc), row),
                  pl.BlockSpec((tm, dc), row),
                  prev_spec,
                  pl.BlockSpec(conv_w.shape, c2),
                  pl.BlockSpec(wo.shape, c2, **single),
                  pl.BlockSpec((1, d), c2),
                  pl.BlockSpec(wu.shape, c2, **single),
                  pl.BlockSpec(wd.shape, c2, **single),
                  pl.BlockSpec((1, d), c2)],
        out_specs=pl.BlockSpec((tm, d), row),
        out_shape=jax.ShapeDtypeStruct((n, d), F32),
        compiler_params=pltpu.CompilerParams(dimension_semantics=("arbitrary",),
                                             vmem_limit_bytes=VMEM_LIMIT),
        name="out_mlp",
    )(x2d, a, bg, u, prev_arg, conv_w, wo, g_mlp, wu, wd, g_final)


def _pick_tile(n, pref):
    t = min(pref, n)
    while n % t:
        t //= 2
    return t


def kernel(x_prompt, x_sample, cache_k, cache_v, cache_kidx, state_conv, page_table, rel_bias,
           g_mix, w_in, conv_w, w_out, g_mlp, w_up, w_down, g_final):
    depth = w_in.shape[0]
    assert depth == 1, "single-layer step"
    batch, seq, d_model = x_prompt.shape
    db, t_new, _ = x_sample.shape
    n_pages = page_table.shape[1]
    ps = cache_k.shape[2]
    past = n_pages * ps
    assert ps == LANES and t_new == SUBLANES

    tq = _pick_tile(seq, 256)
    n_sel_p = min(TOPK_MAX, seq // 4)
    n_sel_s = min(TOPK_MAX, (past + t_new) // 4)
    assert _far_bucket_is_constant(tq + 1, seq) and _far_bucket_is_constant(ps + 1, past + t_new)

    w = w_in[0]
    cq, ck_, cv, cqi, cki, cwi, cbg, ccg, ch = np.cumsum(
        [0, D_ATTN, D_ATTN, D_ATTN, N_IDX_HEADS * IDX_DIM, IDX_DIM, N_IDX_HEADS, D_ATTN, D_ATTN])
    end = ch + D_ATTN
    wt = jnp.pad(w[:, cq:cbg].T, ((0, T_END - cbg), (0, 0))).astype(BF16)
    wn = jnp.concatenate(
        [w[:, ck_:cv], w[:, cbg:end], jnp.pad(w[:, cki:cwi], ((0, 0), (0, LANES - IDX_DIM)))],
        axis=1).astype(BF16)
    wo = w_out[0].astype(BF16)
    wu = w_up[0].astype(BF16)
    wd = w_down[0].astype(BF16)
    gmix = g_mix[0][None]
    gmlp = g_mlp[0][None]
    gfin = g_final[None]
    cw = conv_w[0]

    n_p = batch * seq
    (qT, kT, vT, vTb, qiT, kiT, wiT, kn, kin, bg, u) = _in_proj(
        x_prompt, gmix, wn, wt, _pick_tile(seq, 512))
    bias_p = _bias_tables(rel_bias, tq, tq, (0, tq), -1, 1)
    a_p = _prompt_attention(qT, qiT, wiT, kn, kin, vTb, bias_p, tq, n_sel_p)
    y_prompt = _out_mlp(x_prompt.reshape(n_p, d_model), a_p.reshape(n_p, D_ATTN),
                        bg.reshape(n_p, D_ATTN), u.reshape(n_p, D_ATTN), None, cw, wo, gmlp, wu, wd,
                        gfin, _pick_tile(seq, 512), seq).reshape(batch, seq, d_model)

    def heads_out(t):
        b_, _, s_ = t.shape
        return t.reshape(b_, N_HEADS, HEAD_DIM, s_).transpose(0, 3, 1, 2)[None]

    k_prompt = heads_out(kT)
    v_prompt = heads_out(vT)
    kidx_prompt = kiT.transpose(0, 2, 1)[None]
    conv_prompt = u[:, seq - (CONV_WIDTH - 1):][None]

    ns = db * t_new
    (qTs, kTs, vTs, _, qiTs, kiTs, wiTs, _, _, bgs, us) = [
        t[0] for t in _in_proj(x_sample.reshape(1, ns, d_model), gmix, wn, wt, ns)]
    qi_s = qiTs.reshape(N_IDX_HEADS, IDX_DIM, db, t_new).transpose(2, 0, 3, 1).reshape(
        db, N_IDX_HEADS * t_new, IDX_DIM)
    w_s = wiTs.reshape(N_IDX_HEADS, db, t_new).transpose(1, 0, 2).reshape(db, N_IDX_HEADS * t_new, 1)
    kinew = jnp.pad(kiTs.reshape(IDX_DIM, db, t_new).transpose(1, 0, 2),
                    ((0, 0), (0, 0), (0, LANES - t_new))).astype(BF16)
    q_s = qTs.reshape(N_HEADS, HEAD_DIM, db, t_new).transpose(2, 0, 3, 1)
    eye = jnp.eye(N_HEADS, dtype=q_s.dtype)
    qbd = (q_s[:, :, :, None, :] * eye[None, :, None, :, None]).reshape(
        db, N_HEADS * t_new, N_HEADS * HEAD_DIM)
    pad_new = lambda t: jnp.pad(t.reshape(D_ATTN, db, t_new).transpose(1, 0, 2),
                                ((0, 0), (0, 0), (0, LANES - t_new))).astype(BF16)
    knew = pad_new(kTs)
    vnew = pad_new(vTs)
    kidxT = cache_kidx[0].transpose(0, 2, 1)
    cache_kT = cache_k[0].transpose(0, 2, 3, 1)
    cache_vT = cache_v[0].transpose(0, 2, 3, 1)

    madd = _sample_select(page_table, qi_s, w_s, kinew, kidxT, _pick_tile(n_pages, 32), n_sel_s)
    bias_s = _bias_tables(rel_bias, t_new, LANES, (ps, 0), 1, -1)
    o_s = _sample_attention(page_table, qbd, madd, knew, vnew, bias_s, cache_kT, cache_vT,
                            _pick_tile(n_pages, 16))
    a_s = o_s.transpose(0, 2, 1, 3).reshape(ns, D_ATTN).astype(BF16)
    y_sample = _out_mlp(x_sample.reshape(ns, d_model), a_s, bgs, us, state_conv[0], cw, wo, gmlp,
                        wu, wd, gfin, ns, t_new).reshape(db, t_new, d_model)

    def heads_out_s(t):
        return t.reshape(N_HEADS, HEAD_DIM, db, t_new).transpose(2, 3, 0, 1)[None]

    k_sample = heads_out_s(kTs)
    v_sample = heads_out_s(vTs)
    kidx_sample = kiTs.reshape(IDX_DIM, db, t_new).transpose(1, 2, 0)[None]
    conv_sample = us.reshape(db, t_new, D_ATTN)[:, t_new - (CONV_WIDTH - 1):][None]

    return (y_prompt, y_sample, k_prompt, v_prompt, kidx_prompt, conv_prompt,
            k_sample, v_sample, kidx_sample, conv_sample)
```

```python
import functools
import math

import jax
import jax.numpy as jnp
import numpy as np
from jax import lax
from jax.experimental import pallas as pl
from jax.experimental.pallas import tpu as pltpu

F32 = jnp.float32
BF16 = jnp.bfloat16
I32 = jnp.int32
I16 = jnp.int16
HALF16 = 1 << 15

HEAD_DIM = 64
N_HEADS = 8
N_IDX_HEADS = 8
IDX_DIM = 32
D_ATTN = N_HEADS * HEAD_DIM
TOPK_MAX = 256
CONV_WIDTH = 3
N_BUCKETS = 32
MAX_DISTANCE = 128
EPS = 1e-6
ATTN_SCALE = HEAD_DIM ** -0.5
INDEX_SCALE = (IDX_DIM ** -0.5) * (N_IDX_HEADS ** -0.5)
LOG2E = math.log2(math.e)

LANES = 128
SUBLANES = 8
NEG = -1e30
VMEM_LIMIT = 56 * 1024 * 1024


def _float_key(v):
    b = int(np.array(v, np.float32).view(np.int32))
    return b if b >= 0 else b ^ 0x7FFFFFFF


KEY_LO = _float_key(-np.finfo(np.float32).max)
KEY_HI = _float_key(np.inf)
N_BISECT = 33
LOOKAHEAD = 3
N_SBUF = 4
assert LOOKAHEAD < N_SBUF and N_HEADS % N_SBUF == 0


def _key_to_float(k):
    bits = k ^ ((k >> 31) & 0x7FFFFFFF)
    return lax.bitcast_convert_type(bits, F32)


def _mid(lo, hi):
    return (lo >> 1) + (hi >> 1) + (lo & hi & 1)


def _tree_sum(parts):
    parts = list(parts)
    while len(parts) > 1:
        nxt = [parts[k] + parts[k + 1] for k in range(0, len(parts) - 1, 2)]
        if len(parts) % 2:
            nxt.append(parts[-1])
        parts = nxt
    return parts[0]


def _bias_body(rb_ref, o_ref, *, offs, sa, sb):
    w = pl.program_id(0)
    h = pl.program_id(1)
    shape = o_ref.shape
    a = lax.broadcasted_iota(I32, shape, 0)
    b = lax.broadcasted_iota(I32, shape, 1)
    off = jnp.where(w == 0, offs[0], offs[1])
    dist = off + sa * a + sb * b
    n = jnp.maximum(dist, 0)
    max_exact = N_BUCKETS // 2
    nf = jnp.maximum(n, 1).astype(F32)
    large = max_exact + (jnp.log(nf / max_exact) / math.log(MAX_DISTANCE / max_exact)
                         * (N_BUCKETS - max_exact)).astype(I32)
    large = jnp.minimum(large, N_BUCKETS - 1)
    bucket = jnp.where(n < max_exact, n, large)
    val = jnp.zeros(shape, F32)
    for k in range(N_BUCKETS):
        val = jnp.where(bucket == k, rb_ref[k, h], val)
    o_ref[...] = (val - rb_ref[N_BUCKETS - 1, h]) * LOG2E


def _bias_tables(rel_bias, rows, cols, offs, sa, sb):
    return pl.pallas_call(
        functools.partial(_bias_body, offs=offs, sa=sa, sb=sb),
        grid=(2, N_HEADS),
        in_specs=[pl.BlockSpec(memory_space=pltpu.SMEM)],
        out_specs=pl.BlockSpec((None, None, rows, cols), lambda w, h: (w, h, 0, 0)),
        out_shape=jax.ShapeDtypeStruct((2, N_HEADS, rows, cols), F32),
        name="bias_tables",
    )(rel_bias)


def _far_bucket_is_constant(min_dist, max_dist):
    d = np.arange(min_dist, max_dist + 1, dtype=np.float64)
    me = N_BUCKETS // 2
    b = me + np.floor(np.log(d / me) / math.log(MAX_DISTANCE / me) * (N_BUCKETS - me))
    return bool(np.all(np.minimum(b, N_BUCKETS - 1) == N_BUCKETS - 1)) and min_dist >= me


T_Q, T_K, T_V, T_QI, T_KI, T_WI, T_END = 0, 512, 1024, 1536, 1792, 1824, 1840
N_K, N_BG, N_CG, N_H, N_KI, N_END = 0, 512, 1024, 1536, 2048, 2176


def _inproj_body(x_ref, g_ref, wn_ref, wt_ref,
                 qT_ref, kT_ref, vT_ref, vTb_ref, qiT_ref, kiT_ref, wiT_ref,
                 kn_ref, kin_ref, bg_ref, u_ref):
    x = x_ref[...]
    ms = jnp.mean(x * x, axis=-1, keepdims=True)
    xn = ((x * lax.rsqrt(ms + EPS)) * g_ref[...]).astype(BF16)

    def nat(a, b):
        return jnp.dot(xn, wn_ref[:, a:b], preferred_element_type=F32)

    def tra(a, b):
        return lax.dot_general(wt_ref[a:b, :], xn, (((1,), (1,)), ((), ())),
                               preferred_element_type=F32)

    kn_ref[...] = nat(N_K, N_BG).astype(BF16)
    bg_ref[...] = nat(N_BG, N_CG)
    u_ref[...] = nat(N_CG, N_H) * nat(N_H, N_KI)
    kin_ref[...] = nat(N_KI, N_END).astype(BF16)

    qT_ref[...] = (tra(T_Q, T_K) * (ATTN_SCALE * LOG2E)).astype(BF16)
    kT_ref[...] = tra(T_K, T_V)
    vt = tra(T_V, T_QI)
    vT_ref[...] = vt
    vTb_ref[...] = vt.astype(BF16)
    qiT_ref[...] = tra(T_QI, T_KI).astype(BF16)
    kiT_ref[...] = tra(T_KI, T_WI)
    wiT_ref[...] = tra(T_WI, T_END)[0:N_IDX_HEADS, :] * INDEX_SCALE


def _in_proj(x3, g, wn, wt, tm):
    nb, seq, d = x3.shape
    assert seq % tm == 0
    const = lambda b, i: (0, 0)
    row = lambda b, i: (b, i, 0)
    col = lambda b, i: (b, 0, i)
    di = N_IDX_HEADS * IDX_DIM
    outs = [
        ((nb, D_ATTN, seq), BF16, (None, D_ATTN, tm), col),
        ((nb, D_ATTN, seq), F32, (None, D_ATTN, tm), col),
        ((nb, D_ATTN, seq), F32, (None, D_ATTN, tm), col),
        ((nb, D_ATTN, seq), BF16, (None, D_ATTN, tm), col),
        ((nb, di, seq), BF16, (None, di, tm), col),
        ((nb, IDX_DIM, seq), F32, (None, IDX_DIM, tm), col),
        ((nb, N_IDX_HEADS, seq), F32, (None, N_IDX_HEADS, tm), col),
        ((nb, seq, D_ATTN), BF16, (None, tm, D_ATTN), row),
        ((nb, seq, LANES), BF16, (None, tm, LANES), row),
        ((nb, seq, D_ATTN), F32, (None, tm, D_ATTN), row),
        ((nb, seq, D_ATTN), F32, (None, tm, D_ATTN), row),
    ]
    return pl.pallas_call(
        _inproj_body,
        grid=(nb, seq // tm),
        in_specs=[pl.BlockSpec((None, tm, d), row),
                  pl.BlockSpec((1, d), const),
                  pl.BlockSpec(wn.shape, const),
                  pl.BlockSpec(wt.shape, const)],
        out_specs=[pl.BlockSpec(bs, im) for (_, _, bs, im) in outs],
        out_shape=[jax.ShapeDtypeStruct(s, dt) for (s, dt, _, _) in outs],
        compiler_params=pltpu.CompilerParams(dimension_semantics=("arbitrary", "arbitrary"),
                                             vmem_limit_bytes=VMEM_LIMIT),
        name="in_proj",
    )(x3, g, wn, wt)


def _pattn_body(qT_ref, qiT_ref, wiT_ref, kn_ref, kin_ref, vT_ref, bias_ref, a_ref,
                it_ref, hi_ref, lo_ref, qz_ref, m_ref, l_ref, acc_ref, p_ref, madd_ref, s_ref,
                *, tq, n_sel):
    i = pl.program_id(1)
    nch = i + 1
    ck = tq
    s_tot = kn_ref.shape[0]
    q_idx = i * tq + lax.broadcasted_iota(I32, (ck, tq), 1)
    row_iota = lax.broadcasted_iota(I32, (ck, tq), 0)

    def idx_chunk(j, carry):
        r0 = pl.multiple_of(j * ck, ck)
        ki = kin_ref[pl.ds(r0, ck), :][:, 0:IDX_DIM]
        terms = []
        for h in range(N_IDX_HEADS):
            s = jnp.dot(ki, qiT_ref[h * IDX_DIM:(h + 1) * IDX_DIM, :],
                        preferred_element_type=F32)
            terms.append(wiT_ref[h:h + 1, :] * jnp.maximum(s, 0.0))
        acc = _tree_sum(terms)
        acc = jnp.where(r0 + row_iota <= q_idx, acc, -jnp.inf)
        it_ref[pl.ds(r0, ck), :] = acc
        bits = lax.bitcast_convert_type(acc, I32)
        key = bits ^ ((bits >> 31) & 0x7FFFFFFF)
        hi_ref[pl.ds(r0, ck), :] = (key >> 16).astype(I16)
        lo_ref[pl.ds(r0, ck), :] = ((key & 0xFFFF) - HALF16).astype(I16)
        return carry

    lax.fori_loop(0, nch, idx_chunk, 0)

    def count(pred):
        def body(j, c8):
            r0 = pl.multiple_of(j * ck, ck)
            x = it_ref[pl.ds(r0, ck), :]
            hit = jnp.where(pred(x, r0), 1.0, 0.0)
            return c8 + _tree_sum(hit[r * SUBLANES:(r + 1) * SUBLANES]
                                  for r in range(ck // SUBLANES))
        c8 = lax.fori_loop(0, nch, body, jnp.zeros((SUBLANES, tq), F32))
        return c8.sum(axis=0, keepdims=True)

    rows16 = 2 * SUBLANES

    def count16(ref, thr):
        thr16 = thr.astype(I16)

        def body(j, c16):
            r0 = pl.multiple_of(j * ck, ck)
            v = ref[pl.ds(r0, ck), :]
            hit = jnp.where(v >= thr16, jnp.int16(1), jnp.int16(0))
            return c16 + _tree_sum(hit[r * rows16:(r + 1) * rows16] for r in range(ck // rows16))
        c16 = lax.fori_loop(0, nch, body, jnp.zeros((rows16, tq), I16))
        return c16.astype(I32).sum(axis=0, keepdims=True)

    def bisect16(ref, lo0, hi0, c_lo0, c_hi0, base):
        def step(_, st):
            lo, hi, c_lo, c_hi = st
            mid = (lo + hi) >> 1
            c = base + count16(ref, mid)
            ok = c >= n_sel
            return (jnp.where(ok, mid, lo), jnp.where(ok, hi, mid),
                    jnp.where(ok, c, c_lo), jnp.where(ok, c_hi, c))
        full = lambda v: jnp.full((1, tq), v, I32)
        return lax.fori_loop(0, 16, step, (full(lo0), full(hi0), c_lo0, c_hi0))

    izero = jnp.zeros((1, tq), I32)
    t1, _, c_ge1, c_above = bisect16(hi_ref, KEY_LO >> 16, (KEY_HI >> 16) + 1, izero, izero, izero)
    t1_16 = t1.astype(I16)

    def bucket_chunk(j, carry):
        r0 = pl.multiple_of(j * ck, ck)
        lo_ref[pl.ds(r0, ck), :] = jnp.where(hi_ref[pl.ds(r0, ck), :] == t1_16,
                                             lo_ref[pl.ds(r0, ck), :], jnp.int16(-HALF16))
        return carry

    lax.fori_loop(0, nch, bucket_chunk, 0)
    t2, _, c_ge, c_gt = bisect16(lo_ref, -HALF16, HALF16, c_ge1, c_above, c_above)
    tau = _key_to_float((t1 << 16) + (t2 + HALF16))
    c_ge = c_ge.astype(F32)
    need = n_sel - c_gt.astype(F32)

    p_ref[...] = jnp.full((1, tq), s_tot, I32)

    @pl.when(jnp.max(c_ge) > n_sel)
    def _():
        def tie(_, lohi):
            plo, phi = lohi
            pm = (plo + phi) >> 1
            ok = count(lambda x, r0: (x == tau) & (r0 + row_iota <= pm)) >= need
            return jnp.where(ok, plo, pm), jnp.where(ok, pm, phi)
        n_it = int(math.ceil(math.log2(s_tot))) + 1
        _, phi = lax.fori_loop(0, n_it, tie,
                               (jnp.full((1, tq), -1, I32), jnp.full((1, tq), s_tot - 1, I32)))
        p_ref[...] = phi

    pcut = p_ref[...]

    half = lax.broadcasted_iota(I32, (2 * HEAD_DIM, tq), 0) // HEAD_DIM
    for h in range(N_HEADS):
        pair = qT_ref[(h // 2) * 2 * HEAD_DIM:(h // 2 + 1) * 2 * HEAD_DIM, :]
        qz_ref[h] = jnp.where(half == (h % 2), pair, jnp.zeros_like(pair))
    m_ref[...] = jnp.full(m_ref.shape, NEG, F32)
    l_ref[...] = jnp.zeros(l_ref.shape, F32)
    acc_ref[...] = jnp.zeros(acc_ref.shape, F32)

    def qk(j, h):
        r0 = pl.multiple_of(j * ck, ck)
        kp = kn_ref[pl.ds(r0, ck), (h // 2) * 2 * HEAD_DIM:(h // 2 + 1) * 2 * HEAD_DIM]
        return jnp.dot(kp, qz_ref[h], preferred_element_type=F32)

    for h in range(LOOKAHEAD):
        s_ref[h % N_SBUF] = qk(0, h)

    def attend(j, bias_w, has_next):
        r0 = pl.multiple_of(j * ck, ck)
        x = it_ref[pl.ds(r0, ck), :]
        sel = (x > tau) | ((x == tau) & (r0 + row_iota <= pcut))
        madd_ref[...] = jnp.where(sel, 0.0, NEG)
        ones = jnp.ones((2 * SUBLANES, ck), BF16)
        for h in range(N_HEADS):
            ahead = h + LOOKAHEAD
            if ahead < N_HEADS:
                s_ref[ahead % N_SBUF] = qk(j, ahead)
            elif has_next:
                s_ref[ahead % N_SBUF] = qk(j + 1, ahead - N_HEADS)
            s = s_ref[h % N_SBUF] + madd_ref[...]
            if bias_w is not None:
                s = s + bias_ref[bias_w, h]
            m_old = m_ref[h:h + 1, :]
            m_new = jnp.maximum(m_old, jnp.max(s, axis=0, keepdims=True))
            p = jnp.exp2(s - m_new)
            alpha = jnp.exp2(m_old - m_new)
            vt = vT_ref[h * HEAD_DIM:(h + 1) * HEAD_DIM, pl.ds(r0, ck)]
            pv = jnp.dot(jnp.concatenate([vt, ones], axis=0), p.astype(BF16),
                         preferred_element_type=F32)
            l_ref[h:h + 1, :] = alpha * l_ref[h:h + 1, :] + pv[HEAD_DIM:HEAD_DIM + 1, :]
            acc_ref[h * HEAD_DIM:(h + 1) * HEAD_DIM, :] = (
                alpha * acc_ref[h * HEAD_DIM:(h + 1) * HEAD_DIM, :] + pv[0:HEAD_DIM, :])
            m_ref[h:h + 1, :] = m_new

    def far_chunk(j, carry):
        attend(j, None, True)
        return carry

    lax.fori_loop(0, jnp.maximum(i - 1, 0), far_chunk, 0)

    @pl.when(i >= 1)
    def _():
        attend(i - 1, 1, True)

    attend(i, 0, False)

    for h in range(N_HEADS):
        sl = slice(h * HEAD_DIM, (h + 1) * HEAD_DIM)
        acc_ref[sl, :] = acc_ref[sl, :] / l_ref[h:h + 1, :]
    a_ref[...] = acc_ref[...].T.astype(BF16)


def _prompt_attention(qT, qiT, wiT, kn, kin, vTb, bias, tq, n_sel):
    batch, seq, _ = kn.shape
    nq = seq // tq
    blk = lambda b, i: (b, 0, i)
    per_b = lambda b, i: (b, 0, 0)
    return pl.pallas_call(
        functools.partial(_pattn_body, tq=tq, n_sel=n_sel),
        grid=(batch, nq),
        in_specs=[pl.BlockSpec((None, D_ATTN, tq), blk),
                  pl.BlockSpec((None, N_IDX_HEADS * IDX_DIM, tq), blk),
                  pl.BlockSpec((None, N_IDX_HEADS, tq), blk),
                  pl.BlockSpec((None, seq, D_ATTN), per_b),
                  pl.BlockSpec((None, seq, LANES), per_b),
                  pl.BlockSpec((None, D_ATTN, seq), per_b),
                  pl.BlockSpec(bias.shape, lambda b, i: (0, 0, 0, 0))],
        out_specs=pl.BlockSpec((None, tq, D_ATTN), lambda b, i: (b, i, 0)),
        out_shape=jax.ShapeDtypeStruct((batch, seq, D_ATTN), BF16),
        scratch_shapes=[pltpu.VMEM((seq, tq), F32),
                        pltpu.VMEM((seq, tq), I16),
                        pltpu.VMEM((seq, tq), I16),
                        pltpu.VMEM((N_HEADS, 2 * HEAD_DIM, tq), BF16),
                        pltpu.VMEM((N_HEADS, tq), F32),
                        pltpu.VMEM((N_HEADS, tq), F32),
                        pltpu.VMEM((D_ATTN, tq), F32),
                        pltpu.VMEM((1, tq), I32),
                        pltpu.VMEM((tq, tq), F32),
                        pltpu.VMEM((N_SBUF, tq, tq), F32)],
        compiler_params=pltpu.CompilerParams(dimension_semantics=("arbitrary", "arbitrary"),
                                             vmem_limit_bytes=VMEM_LIMIT),
        name="prompt_attention",
    )(qT, qiT, wiT, kn, kin, vTb, bias)


def _sidx_body(pt_ref, qi_ref, w_ref, kinew_ref, *rest, pg, n_pages, t_new, n_sel):
    pages = rest[:pg]
    madd_ref = rest[pg]
    it_ref, p_ref = rest[pg + 1:]
    g = pl.program_id(1)
    ps = pages[0].shape[-1]
    past = n_pages * ps
    tot = past + LANES
    n_slab = tot // LANES
    qi = qi_ref[...]
    w = w_ref[...]

    def scores(kt):
        s = jnp.dot(qi, kt.astype(BF16), preferred_element_type=F32)
        r = jnp.maximum(s, 0.0) * w
        return _tree_sum(r[h * t_new:(h + 1) * t_new] for h in range(N_IDX_HEADS))

    kt = jnp.concatenate([p[...] for p in pages], axis=-1)
    c0 = pl.multiple_of(g * (pg * ps), pg * ps)
    it_ref[:, pl.ds(c0, pg * ps)] = scores(kt)

    @pl.when(g == pl.num_programs(1) - 1)
    def _():
        sn = scores(kinew_ref[...].astype(F32))
        tq_i = lax.broadcasted_iota(I32, (t_new, LANES), 0)
        lane_i = lax.broadcasted_iota(I32, (t_new, LANES), 1)
        it_ref[:, past:tot] = jnp.where(lane_i <= tq_i, sn, -jnp.inf)

        def count(pred):
            parts = []
            for sl in range(n_slab):
                x = it_ref[:, sl * LANES:(sl + 1) * LANES]
                parts.append(jnp.where(pred(x, sl * LANES), 1.0, 0.0))
            return jnp.sum(_tree_sum(parts), axis=1, keepdims=True)

        def bis(_, st):
            lo, hi, c_lo, c_hi = st
            mid = _mid(lo, hi)
            midf = _key_to_float(mid)
            c = count(lambda x, c0_: x >= midf)
            ok = c >= n_sel
            return (jnp.where(ok, mid, lo), jnp.where(ok, hi, mid),
                    jnp.where(ok, c, c_lo), jnp.where(ok, c_hi, c))

        zero = jnp.zeros((t_new, 1), F32)
        lo, _, c_ge, c_gt = lax.fori_loop(
            0, N_BISECT, bis,
            (jnp.full((t_new, 1), KEY_LO, I32), jnp.full((t_new, 1), KEY_HI, I32), zero, zero))
        tau = _key_to_float(lo)
        need = n_sel - c_gt
        p_ref[...] = jnp.full((t_new, 1), tot, I32)

        @pl.when(jnp.max(c_ge) > n_sel)
        def _():
            def tie(_, lohi):
                plo, phi = lohi
                pm = (plo + phi) >> 1
                ok = count(lambda x, c0_: (x == tau) & (c0_ + lane_i <= pm)) >= need
                return jnp.where(ok, plo, pm), jnp.where(ok, pm, phi)
            n_it = int(math.ceil(math.log2(tot))) + 1
            _, phi = lax.fori_loop(0, n_it, tie, (jnp.full((t_new, 1), -1, I32),
                                                  jnp.full((t_new, 1), tot - 1, I32)))
            p_ref[...] = phi

        pcut = p_ref[...]
        for sl in range(n_slab):
            x = it_ref[:, sl * LANES:(sl + 1) * LANES]
            sel = (x > tau) | ((x == tau) & (sl * LANES + lane_i <= pcut))
            madd_ref[:, sl * LANES:(sl + 1) * LANES] = jnp.where(sel, 0.0, NEG)


def _sample_select(page_table, qi_s, w_s, kinew, kidxT, pg, n_sel):
    db, n_pages = page_table.shape
    ps = kidxT.shape[-1]
    t_new = qi_s.shape[1] // N_IDX_HEADS
    tot = n_pages * ps + LANES
    assert n_pages % pg == 0 and ps == LANES

    def page_spec(u):
        return pl.BlockSpec((None, IDX_DIM, ps), lambda b, g, pt: (pt[b, g * pg + u], 0, 0))

    grid_spec = pltpu.PrefetchScalarGridSpec(
        num_scalar_prefetch=1,
        grid=(db, n_pages // pg),
        in_specs=[pl.BlockSpec((None,) + qi_s.shape[1:], lambda b, g, pt: (b, 0, 0)),
                  pl.BlockSpec((None,) + w_s.shape[1:], lambda b, g, pt: (b, 0, 0)),
                  pl.BlockSpec((None,) + kinew.shape[1:], lambda b, g, pt: (b, 0, 0))]
                 + [page_spec(u) for u in range(pg)],
        out_specs=pl.BlockSpec((None, t_new, tot), lambda b, g, pt: (b, 0, 0)),
        scratch_shapes=[pltpu.VMEM((t_new, tot), F32), pltpu.VMEM((t_new, 1), I32)])
    return pl.pallas_call(
        functools.partial(_sidx_body, pg=pg, n_pages=n_pages, t_new=t_new, n_sel=n_sel),
        grid_spec=grid_spec,
        out_shape=jax.ShapeDtypeStruct((db, t_new, tot), F32),
        compiler_params=pltpu.CompilerParams(dimension_semantics=("arbitrary", "arbitrary"),
                                             vmem_limit_bytes=VMEM_LIMIT),
        name="sample_select",
    )(page_table, qi_s, w_s, kinew, *([kidxT] * pg))


def _sattn_body(pt_ref, qbd_ref, madd_ref, maddn_ref, knew_ref, vnew_ref, bias_ref, *rest, pg):
    kpages = rest[:pg]
    vpages = rest[pg:2 * pg]
    o_ref = rest[2 * pg]
    m_ref, l_ref, acc_ref = rest[2 * pg + 1:]
    g = pl.program_id(1)
    last = pl.num_programs(1) - 1
    qbd = qbd_ref[...]
    ps = kpages[0].shape[-1]
    t_new = madd_ref.shape[0]
    hd = N_HEADS * HEAD_DIM

    @pl.when(g == 0)
    def _():
        m_ref[...] = jnp.full(m_ref.shape, NEG, F32)
        l_ref[...] = jnp.zeros(l_ref.shape, F32)
        acc_ref[...] = jnp.zeros(acc_ref.shape, F32)

    def flash(s, vt):
        m_old = m_ref[...]
        m_new = jnp.maximum(m_old, jnp.max(s, axis=-1, keepdims=True))
        p = jnp.exp2(s - m_new)
        alpha = jnp.exp2(m_old - m_new)
        l_ref[...] = alpha * l_ref[...] + jnp.sum(p, axis=-1, keepdims=True)
        pv = lax.dot_general(p.astype(BF16), vt, (((1,), (1,)), ((), ())),
                             preferred_element_type=F32)
        acc_ref[...] = alpha * acc_ref[...] + pv
        m_ref[...] = m_new

    def add_rows(s, add):
        n = s.shape[-1]
        return (s.reshape(N_HEADS, t_new, n) + add[None]).reshape(N_HEADS * t_new, n)

    kt = jnp.concatenate([kp[...].reshape(hd, ps) for kp in kpages], axis=-1).astype(BF16)
    vt = jnp.concatenate([vp[...].reshape(hd, ps) for vp in vpages], axis=-1).astype(BF16)
    s = jnp.dot(qbd, kt, preferred_element_type=F32)
    s = add_rows(s, madd_ref[...])
    is_last = jnp.where(g == last, 1.0, 0.0)
    tail = s[:, (pg - 1) * ps:] + is_last * bias_ref[0].reshape(N_HEADS * t_new, ps)
    s = jnp.concatenate([s[:, :(pg - 1) * ps], tail], axis=-1)
    flash(s, vt)

    @pl.when(g == last)
    def _():
        sn = jnp.dot(qbd, knew_ref[...], preferred_element_type=F32)
        sn = add_rows(sn + bias_ref[1].reshape(N_HEADS * t_new, LANES), maddn_ref[...])
        flash(sn, vnew_ref[...])
        out = acc_ref[...] / l_ref[...]
        for h in range(N_HEADS):
            o_ref[h] = out[h * t_new:(h + 1) * t_new, h * HEAD_DIM:(h + 1) * HEAD_DIM]


def _sample_attention(page_table, qbd, madd, knew, vnew, bias_s, cache_kT, cache_vT, pg):
    db, n_pages = page_table.shape
    ps = cache_kT.shape[-1]
    t_new = madd.shape[1]
    assert n_pages % pg == 0

    def page_spec(u):
        return pl.BlockSpec((None, N_HEADS, HEAD_DIM, ps),
                            lambda b, g, pt: (pt[b, g * pg + u], 0, 0, 0))

    per_b = lambda b, g, pt: (b, 0, 0)
    grid_spec = pltpu.PrefetchScalarGridSpec(
        num_scalar_prefetch=1,
        grid=(db, n_pages // pg),
        in_specs=[pl.BlockSpec((None,) + qbd.shape[1:], per_b),
                  pl.BlockSpec((None, t_new, pg * ps), lambda b, g, pt: (b, 0, g)),
                  pl.BlockSpec((None, t_new, LANES), lambda b, g, pt: (b, 0, n_pages * ps // LANES)),
                  pl.BlockSpec((None,) + knew.shape[1:], per_b),
                  pl.BlockSpec((None,) + vnew.shape[1:], per_b),
                  pl.BlockSpec(bias_s.shape, lambda b, g, pt: (0, 0, 0, 0))]
                 + [page_spec(u) for u in range(pg)] * 2,
        out_specs=pl.BlockSpec((None, N_HEADS, t_new, HEAD_DIM), lambda b, g, pt: (b, 0, 0, 0)),
        scratch_shapes=[pltpu.VMEM((N_HEADS * t_new, 1), F32),
                        pltpu.VMEM((N_HEADS * t_new, 1), F32),
                        pltpu.VMEM((N_HEADS * t_new, N_HEADS * HEAD_DIM), F32)])
    return pl.pallas_call(
        functools.partial(_sattn_body, pg=pg),
        grid_spec=grid_spec,
        out_shape=jax.ShapeDtypeStruct((db, N_HEADS, t_new, HEAD_DIM), F32),
        compiler_params=pltpu.CompilerParams(dimension_semantics=("arbitrary", "arbitrary"),
                                             vmem_limit_bytes=VMEM_LIMIT),
        name="sample_attention",
    )(page_table, qbd, madd, madd, knew, vnew, bias_s, *([cache_kT] * pg), *([cache_vT] * pg))


def _outmlp_body(x_ref, a_ref, bg_ref, u_ref, prev_ref, cw_ref, wo_ref, gm_ref, wu_ref, wd_ref,
                 gf_ref, y_ref, *, seq_len, ff_chunk):
    tm = x_ref.shape[0]
    u = u_ref[...]
    w0 = cw_ref[0:1, :]
    w1 = cw_ref[1:2, :]
    w2 = cw_ref[2:3, :]
    if seq_len >= tm:
        first = (pl.program_id(0) % (seq_len // tm)) == 0
        halo = prev_ref[...] * jnp.where(first, 0.0, 1.0)
        row = lax.broadcasted_iota(I32, u.shape, 0)
        um1 = jnp.where(row == 0, halo[7:8, :], pltpu.roll(u, 1, 0))
        um2 = jnp.where(row == 0, halo[6:7, :],
                        jnp.where(row == 1, halo[7:8, :], pltpu.roll(u, 2, 0)))
    else:
        nseq = tm // seq_len
        u3 = u.reshape(nseq, seq_len, u.shape[-1])
        up = jnp.concatenate([prev_ref[...], u3], axis=1)
        um1 = up[:, 1:1 + seq_len].reshape(u.shape)
        um2 = up[:, 0:seq_len].reshape(u.shape)
    y = um2 * w0
    y = y + um1 * w1
    y = y + u * w2
    b = (bg_ref[...] * y).astype(BF16)
    ab = jnp.concatenate([a_ref[...], b], axis=-1)
    x1 = x_ref[...] + jnp.dot(ab, wo_ref[...], preferred_element_type=F32)
    ms = jnp.mean(x1 * x1, axis=-1, keepdims=True)
    hn = ((x1 * lax.rsqrt(ms + EPS)) * gm_ref[...]).astype(BF16)
    acc = jnp.zeros(x1.shape, F32)
    d_ff = wu_ref.shape[1]
    for c in range(d_ff // ff_chunk):
        sl = slice(c * ff_chunk, (c + 1) * ff_chunk)
        up_c = jnp.dot(hn, wu_ref[:, sl], preferred_element_type=F32)
        r = jnp.maximum(up_c, 0.0)
        acc = acc + jnp.dot((r * r).astype(BF16), wd_ref[sl, :], preferred_element_type=F32)
    x2 = x1 + acc
    ms2 = jnp.mean(x2 * x2, axis=-1, keepdims=True)
    y_ref[...] = (x2 * lax.rsqrt(ms2 + EPS)) * gf_ref[...]


def _out_mlp(x2d, a, bg, u, prev, conv_w, wo, g_mlp, wu, wd, g_final, tm, seq_len):
    n, d = x2d.shape
    dc = bg.shape[1]
    row = lambda i: (i, 0)
    c2 = lambda i: (0, 0)
    if seq_len >= tm:
        prev_spec = pl.BlockSpec((SUBLANES, dc),
                                 lambda i: (jnp.maximum(i * (tm // SUBLANES) - 1, 0), 0))
        prev_arg = u
    else:
        nseq = tm // seq_len
        prev_spec = pl.BlockSpec((nseq,) + prev.shape[1:], lambda i: (i, 0, 0))
        prev_arg = prev
    single = dict(pipeline_mode=pl.Buffered(1))
    return pl.pallas_call(
        functools.partial(_outmlp_body, seq_len=seq_len, ff_chunk=1024),
        grid=(n // tm,),
        in_specs=[pl.BlockSpec((tm, d), row),
                  pl.BlockSpec((tm, a.shape[1]), row),
                  pl.BlockSpec((tm, dc), row),
                  pl.BlockSpec((tm, dc), row),
                  prev_spec,
                  pl.BlockSpec(conv_w.shape, c2),
                  pl.BlockSpec(wo.shape, c2, **single),
                  pl.BlockSpec((1, d), c2),
                  pl.BlockSpec(wu.shape, c2, **single),
                  pl.BlockSpec(wd.shape, c2, **single),
                  pl.BlockSpec((1, d), c2)],
        out_specs=pl.BlockSpec((tm, d), row),
        out_shape=jax.ShapeDtypeStruct((n, d), F32),
        compiler_params=pltpu.CompilerParams(dimension_semantics=("arbitrary",),
                                             vmem_limit_bytes=VMEM_LIMIT),
        name="out_mlp",
    )(x2d, a, bg, u, prev_arg, conv_w, wo, g_mlp, wu, wd, g_final)


def _pick_tile(n, pref):
    t = min(pref, n)
    while n % t:
        t //= 2
    return t


def kernel(x_prompt, x_sample, cache_k, cache_v, cache_kidx, state_conv, page_table, rel_bias,
           g_mix, w_in, conv_w, w_out, g_mlp, w_up, w_down, g_final):
    depth = w_in.shape[0]
    assert depth == 1, "single-layer step"
    batch, seq, d_model = x_prompt.shape
    db, t_new, _ = x_sample.shape
    n_pages = page_table.shape[1]
    ps = cache_k.shape[2]
    past = n_pages * ps
    assert ps == LANES and t_new == SUBLANES

    tq = _pick_tile(seq, 256)
    n_sel_p = min(TOPK_MAX, seq // 4)
    n_sel_s = min(TOPK_MAX, (past + t_new) // 4)
    assert _far_bucket_is_constant(tq + 1, seq) and _far_bucket_is_constant(ps + 1, past + t_new)

    w = w_in[0]
    cq, ck_, cv, cqi, cki, cwi, cbg, ccg, ch = np.cumsum(
        [0, D_ATTN, D_ATTN, D_ATTN, N_IDX_HEADS * IDX_DIM, IDX_DIM, N_IDX_HEADS, D_ATTN, D_ATTN])
    end = ch + D_ATTN
    wt = jnp.pad(w[:, cq:cbg].T, ((0, T_END - cbg), (0, 0))).astype(BF16)
    wn = jnp.concatenate(
        [w[:, ck_:cv], w[:, cbg:end], jnp.pad(w[:, cki:cwi], ((0, 0), (0, LANES - IDX_DIM)))],
        axis=1).astype(BF16)
    wo = w_out[0].astype(BF16)
    wu = w_up[0].astype(BF16)
    wd = w_down[0].astype(BF16)
    gmix = g_mix[0][None]
    gmlp = g_mlp[0][None]
    gfin = g_final[None]
    cw = conv_w[0]

    n_p = batch * seq
    (qT, kT, vT, vTb, qiT, kiT, wiT, kn, kin, bg, u) = _in_proj(
        x_prompt, gmix, wn, wt, _pick_tile(seq, 512))
    bias_p = _bias_tables(rel_bias, tq, tq, (0, tq), -1, 1)
    a_p = _prompt_attention(qT, qiT, wiT, kn, kin, vTb, bias_p, tq, n_sel_p)
    y_prompt = _out_mlp(x_prompt.reshape(n_p, d_model), a_p.reshape(n_p, D_ATTN),
                        bg.reshape(n_p, D_ATTN), u.reshape(n_p, D_ATTN), None, cw, wo, gmlp, wu, wd,
                        gfin, _pick_tile(seq, 512), seq).reshape(batch, seq, d_model)

    def heads_out(t):
        b_, _, s_ = t.shape
        return t.reshape(b_, N_HEADS, HEAD_DIM, s_).transpose(0, 3, 1, 2)[None]

    k_prompt = heads_out(kT)
    v_prompt = heads_out(vT)
    kidx_prompt = kiT.transpose(0, 2, 1)[None]
    conv_prompt = u[:, seq - (CONV_WIDTH - 1):][None]

    ns = db * t_new
    (qTs, kTs, vTs, _, qiTs, kiTs, wiTs, _, _, bgs, us) = [
        t[0] for t in _in_proj(x_sample.reshape(1, ns, d_model), gmix, wn, wt, ns)]
    qi_s = qiTs.reshape(N_IDX_HEADS, IDX_DIM, db, t_new).transpose(2, 0, 3, 1).reshape(
        db, N_IDX_HEADS * t_new, IDX_DIM)
    w_s = wiTs.reshape(N_IDX_HEADS, db, t_new).transpose(1, 0, 2).reshape(db, N_IDX_HEADS * t_new, 1)
    kinew = jnp.pad(kiTs.reshape(IDX_DIM, db, t_new).transpose(1, 0, 2),
                    ((0, 0), (0, 0), (0, LANES - t_new))).astype(BF16)
    q_s = qTs.reshape(N_HEADS, HEAD_DIM, db, t_new).transpose(2, 0, 3, 1)
    eye = jnp.eye(N_HEADS, dtype=q_s.dtype)
    qbd = (q_s[:, :, :, None, :] * eye[None, :, None, :, None]).reshape(
        db, N_HEADS * t_new, N_HEADS * HEAD_DIM)
    pad_new = lambda t: jnp.pad(t.reshape(D_ATTN, db, t_new).transpose(1, 0, 2),
                                ((0, 0), (0, 0), (0, LANES - t_new))).astype(BF16)
    knew = pad_new(kTs)
    vnew = pad_new(vTs)
    kidxT = cache_kidx[0].transpose(0, 2, 1)
    cache_kT = cache_k[0].transpose(0, 2, 3, 1)
    cache_vT = cache_v[0].transpose(0, 2, 3, 1)

    madd = _sample_select(page_table, qi_s, w_s, kinew, kidxT, _pick_tile(n_pages, 32), n_sel_s)
    bias_s = _bias_tables(rel_bias, t_new, LANES, (ps, 0), 1, -1)
    o_s = _sample_attention(page_table, qbd, madd, knew, vnew, bias_s, cache_kT, cache_vT,
                            _pick_tile(n_pages, 16))
    a_s = o_s.transpose(0, 2, 1, 3).reshape(ns, D_ATTN).astype(BF16)
    y_sample = _out_mlp(x_sample.reshape(ns, d_model), a_s, bgs, us, state_conv[0], cw, wo, gmlp,
                        wu, wd, gfin, ns, t_new).reshape(db, t_new, d_model)

    def heads_out_s(t):
        return t.reshape(N_HEADS, HEAD_DIM, db, t_new).transpose(2, 3, 0, 1)[None]

    k_sample = heads_out_s(kTs)
    v_sample = heads_out_s(vTs)
    kidx_sample = kiTs.reshape(IDX_DIM, db, t_new).transpose(1, 2, 0)[None]
    conv_sample = us.reshape(db, t_new, D_ATTN)[:, t_new - (CONV_WIDTH - 1):][None]

    return (y_prompt, y_sample, k_prompt, v_prompt, kidx_prompt, conv_prompt,
            k_sample, v_sample, kidx_sample, conv_sample)
```

```python
import functools
import math

import jax
import jax.numpy as jnp
import numpy as np
from jax import lax
from jax.experimental import pallas as pl
from jax.experimental.pallas import tpu as pltpu

F32 = jnp.float32
BF16 = jnp.bfloat16
I32 = jnp.int32
I16 = jnp.int16
HALF16 = 1 << 15

HEAD_DIM = 64
N_HEADS = 8
N_IDX_HEADS = 8
IDX_DIM = 32
D_ATTN = N_HEADS * HEAD_DIM
TOPK_MAX = 256
CONV_WIDTH = 3
N_BUCKETS = 32
MAX_DISTANCE = 128
EPS = 1e-6
ATTN_SCALE = HEAD_DIM ** -0.5
INDEX_SCALE = (IDX_DIM ** -0.5) * (N_IDX_HEADS ** -0.5)
LOG2E = math.log2(math.e)

LANES = 128
SUBLANES = 8
NEG = -1e30
VMEM_LIMIT = 56 * 1024 * 1024


def _float_key(v):
    b = int(np.array(v, np.float32).view(np.int32))
    return b if b >= 0 else b ^ 0x7FFFFFFF


KEY_LO = _float_key(-np.finfo(np.float32).max)
KEY_HI = _float_key(np.inf)
N_BISECT = 33
N_SBUF = N_HEADS
LOOKAHEAD = N_HEADS - 1
assert N_IDX_HEADS == N_HEADS and 0 < LOOKAHEAD < N_SBUF


def _key_to_float(k):
    bits = k ^ ((k >> 31) & 0x7FFFFFFF)
    return lax.bitcast_convert_type(bits, F32)


def _mid(lo, hi):
    return (lo >> 1) + (hi >> 1) + (lo & hi & 1)


def _tree_sum(parts):
    parts = list(parts)
    while len(parts) > 1:
        nxt = [parts[k] + parts[k + 1] for k in range(0, len(parts) - 1, 2)]
        if len(parts) % 2:
            nxt.append(parts[-1])
        parts = nxt
    return parts[0]


def _bias_body(rb_ref, o_ref, *, offs, sa, sb):
    w = pl.program_id(0)
    h = pl.program_id(1)
    shape = o_ref.shape
    a = lax.broadcasted_iota(I32, shape, 0)
    b = lax.broadcasted_iota(I32, shape, 1)
    off = jnp.where(w == 0, offs[0], offs[1])
    dist = off + sa * a + sb * b
    n = jnp.maximum(dist, 0)
    max_exact = N_BUCKETS // 2
    nf = jnp.maximum(n, 1).astype(F32)
    large = max_exact + (jnp.log(nf / max_exact) / math.log(MAX_DISTANCE / max_exact)
                         * (N_BUCKETS - max_exact)).astype(I32)
    large = jnp.minimum(large, N_BUCKETS - 1)
    bucket = jnp.where(n < max_exact, n, large)
    val = jnp.zeros(shape, F32)
    for k in range(N_BUCKETS):
        val = jnp.where(bucket == k, rb_ref[k, h], val)
    o_ref[...] = (val - rb_ref[N_BUCKETS - 1, h]) * LOG2E


def _bias_tables(rel_bias, rows, cols, offs, sa, sb):
    return pl.pallas_call(
        functools.partial(_bias_body, offs=offs, sa=sa, sb=sb),
        grid=(2, N_HEADS),
        in_specs=[pl.BlockSpec(memory_space=pltpu.SMEM)],
        out_specs=pl.BlockSpec((None, None, rows, cols), lambda w, h: (w, h, 0, 0)),
        out_shape=jax.ShapeDtypeStruct((2, N_HEADS, rows, cols), F32),
        name="bias_tables",
    )(rel_bias)


def _far_bucket_is_constant(min_dist, max_dist):
    d = np.arange(min_dist, max_dist + 1, dtype=np.float64)
    me = N_BUCKETS // 2
    b = me + np.floor(np.log(d / me) / math.log(MAX_DISTANCE / me) * (N_BUCKETS - me))
    return bool(np.all(np.minimum(b, N_BUCKETS - 1) == N_BUCKETS - 1)) and min_dist >= me


T_Q, T_K, T_V, T_QI, T_KI, T_WI, T_END = 0, 512, 1024, 1536, 1792, 1824, 1840
N_K, N_BG, N_CG, N_H, N_KI, N_END = 0, 512, 1024, 1536, 2048, 2176


def _inproj_body(x_ref, g_ref, wn_ref, wt_ref,
                 qT_ref, kT_ref, vT_ref, vTb_ref, qiT_ref, kiT_ref, wiT_ref,
                 kn_ref, kin_ref, bg_ref, u_ref):
    x = x_ref[...]
    ms = jnp.mean(x * x, axis=-1, keepdims=True)
    xn = ((x * lax.rsqrt(ms + EPS)) * g_ref[...]).astype(BF16)

    def nat(a, b):
        return jnp.dot(xn, wn_ref[:, a:b], preferred_element_type=F32)

    def tra(a, b):
        return lax.dot_general(wt_ref[a:b, :], xn, (((1,), (1,)), ((), ())),
                               preferred_element_type=F32)

    kn_ref[...] = nat(N_K, N_BG).astype(BF16)
    bg_ref[...] = nat(N_BG, N_CG)
    u_ref[...] = nat(N_CG, N_H) * nat(N_H, N_KI)
    kin_ref[...] = nat(N_KI, N_END).astype(BF16)

    qT_ref[...] = (tra(T_Q, T_K) * (ATTN_SCALE * LOG2E)).astype(BF16)
    kT_ref[...] = tra(T_K, T_V)
    vt = tra(T_V, T_QI)
    vT_ref[...] = vt
    vTb_ref[...] = vt.astype(BF16)
    qiT_ref[...] = tra(T_QI, T_KI).astype(BF16)
    kiT_ref[...] = tra(T_KI, T_WI)
    wiT_ref[...] = tra(T_WI, T_END)[0:N_IDX_HEADS, :] * INDEX_SCALE


def _in_proj(x3, g, wn, wt, tm):
    nb, seq, d = x3.shape
    assert seq % tm == 0
    const = lambda b, i: (0, 0)
    row = lambda b, i: (b, i, 0)
    col = lambda b, i: (b, 0, i)
    di = N_IDX_HEADS * IDX_DIM
    outs = [
        ((nb, D_ATTN, seq), BF16, (None, D_ATTN, tm), col),
        ((nb, D_ATTN, seq), F32, (None, D_ATTN, tm), col),
        ((nb, D_ATTN, seq), F32, (None, D_ATTN, tm), col),
        ((nb, D_ATTN, seq), BF16, (None, D_ATTN, tm), col),
        ((nb, di, seq), BF16, (None, di, tm), col),
        ((nb, IDX_DIM, seq), F32, (None, IDX_DIM, tm), col),
        ((nb, N_IDX_HEADS, seq), F32, (None, N_IDX_HEADS, tm), col),
        ((nb, seq, D_ATTN), BF16, (None, tm, D_ATTN), row),
        ((nb, seq, LANES), BF16, (None, tm, LANES), row),
        ((nb, seq, D_ATTN), F32, (None, tm, D_ATTN), row),
        ((nb, seq, D_ATTN), F32, (None, tm, D_ATTN), row),
    ]
    return pl.pallas_call(
        _inproj_body,
        grid=(nb, seq // tm),
        in_specs=[pl.BlockSpec((None, tm, d), row),
                  pl.BlockSpec((1, d), const),
                  pl.BlockSpec(wn.shape, const),
                  pl.BlockSpec(wt.shape, const)],
        out_specs=[pl.BlockSpec(bs, im) for (_, _, bs, im) in outs],
        out_shape=[jax.ShapeDtypeStruct(s, dt) for (s, dt, _, _) in outs],
        compiler_params=pltpu.CompilerParams(dimension_semantics=("arbitrary", "arbitrary"),
                                             vmem_limit_bytes=VMEM_LIMIT),
        name="in_proj",
    )(x3, g, wn, wt)


def _pattn_body(qT_ref, qiT_ref, wiT_ref, kn_ref, kin_ref, vT_ref, bias_ref, a_ref,
                it_ref, hi_ref, lo_ref, qz_ref, m_ref, l_ref, acc_ref, p_ref, madd_ref, s_ref,
                *, tq, n_sel):
    i = pl.program_id(1)
    nch = i + 1
    ck = tq
    s_tot = kn_ref.shape[0]
    q_idx = i * tq + lax.broadcasted_iota(I32, (ck, tq), 1)
    row_iota = lax.broadcasted_iota(I32, (ck, tq), 0)

    def idx_dot(j, h):
        r0 = pl.multiple_of(j * ck, ck)
        ki = kin_ref[pl.ds(r0, ck), :][:, 0:IDX_DIM]
        return jnp.dot(ki, qiT_ref[h * IDX_DIM:(h + 1) * IDX_DIM, :],
                       preferred_element_type=F32)

    for h in range(LOOKAHEAD):
        s_ref[h] = idx_dot(0, h)

    def idx_chunk(j, carry):
        r0 = pl.multiple_of(j * ck, ck)
        jn = jnp.minimum(j + 1, nch - 1)
        terms = []
        for h in range(N_IDX_HEADS):
            ahead = h + LOOKAHEAD
            if ahead < N_IDX_HEADS:
                s_ref[ahead] = idx_dot(j, ahead)
            else:
                s_ref[ahead - N_IDX_HEADS] = idx_dot(jn, ahead - N_IDX_HEADS)
            terms.append(wiT_ref[h:h + 1, :] * jnp.maximum(s_ref[h], 0.0))
        acc = _tree_sum(terms)
        acc = jnp.where(r0 + row_iota <= q_idx, acc, -jnp.inf)
        it_ref[pl.ds(r0, ck), :] = acc
        bits = lax.bitcast_convert_type(acc, I32)
        key = bits ^ ((bits >> 31) & 0x7FFFFFFF)
        hi_ref[pl.ds(r0, ck), :] = (key >> 16).astype(I16)
        lo_ref[pl.ds(r0, ck), :] = ((key & 0xFFFF) - HALF16).astype(I16)
        return carry

    lax.fori_loop(0, nch, idx_chunk, 0)

    def count(pred):
        def body(j, c8):
            r0 = pl.multiple_of(j * ck, ck)
            x = it_ref[pl.ds(r0, ck), :]
            hit = jnp.where(pred(x, r0), 1.0, 0.0)
            return c8 + _tree_sum(hit[r * SUBLANES:(r + 1) * SUBLANES]
                                  for r in range(ck // SUBLANES))
        c8 = lax.fori_loop(0, nch, body, jnp.zeros((SUBLANES, tq), F32))
        return c8.sum(axis=0, keepdims=True)

    rows16 = 2 * SUBLANES
    ck2 = 2 * ck
    npair = (nch + 1) >> 1

    @pl.when((nch & 1) == 1)
    def _():
        pad0 = pl.multiple_of(nch * ck, ck)
        hi_ref[pl.ds(pad0, ck), :] = jnp.full((ck, tq), -HALF16, I16)
        lo_ref[pl.ds(pad0, ck), :] = jnp.full((ck, tq), -HALF16, I16)

    def count16(ref, thr):
        thr16 = thr.astype(I16)

        def body(t, c16):
            r0 = pl.multiple_of(t * ck2, ck2)
            v = ref[pl.ds(r0, ck2), :]
            hit = jnp.where(v >= thr16, jnp.int16(1), jnp.int16(0))
            return c16 + _tree_sum(hit[r * rows16:(r + 1) * rows16] for r in range(ck2 // rows16))
        c16 = lax.fori_loop(0, npair, body, jnp.zeros((rows16, tq), I16))
        return c16.astype(I32).sum(axis=0, keepdims=True)

    def bisect16(ref, lo0, hi0, c_lo0, c_hi0, base):
        def step(_, st):
            lo, hi, c_lo, c_hi = st
            mid = (lo + hi) >> 1
            c = base + count16(ref, mid)
            ok = c >= n_sel
            return (jnp.where(ok, mid, lo), jnp.where(ok, hi, mid),
                    jnp.where(ok, c, c_lo), jnp.where(ok, c_hi, c))
        full = lambda v: jnp.full((1, tq), v, I32)
        return lax.fori_loop(0, 16, step, (full(lo0), full(hi0), c_lo0, c_hi0))

    izero = jnp.zeros((1, tq), I32)
    t1, _, c_ge1, c_above = bisect16(hi_ref, KEY_LO >> 16, (KEY_HI >> 16) + 1, izero, izero, izero)
    t1_16 = t1.astype(I16)

    def bucket_chunk(t, carry):
        r0 = pl.multiple_of(t * ck2, ck2)
        lo_ref[pl.ds(r0, ck2), :] = jnp.where(hi_ref[pl.ds(r0, ck2), :] == t1_16,
                                              lo_ref[pl.ds(r0, ck2), :], jnp.int16(-HALF16))
        return carry

    lax.fori_loop(0, npair, bucket_chunk, 0)
    t2, _, c_ge, c_gt = bisect16(lo_ref, -HALF16, HALF16, c_ge1, c_above, c_above)
    tau = _key_to_float((t1 << 16) + (t2 + HALF16))
    c_ge = c_ge.astype(F32)
    need = n_sel - c_gt.astype(F32)

    p_ref[...] = jnp.full((1, tq), s_tot, I32)

    @pl.when(jnp.max(c_ge) > n_sel)
    def _():
        def tie(_, lohi):
            plo, phi = lohi
            pm = (plo + phi) >> 1
            ok = count(lambda x, r0: (x == tau) & (r0 + row_iota <= pm)) >= need
            return jnp.where(ok, plo, pm), jnp.where(ok, pm, phi)
        n_it = int(math.ceil(math.log2(s_tot))) + 1
        _, phi = lax.fori_loop(0, n_it, tie,
                               (jnp.full((1, tq), -1, I32), jnp.full((1, tq), s_tot - 1, I32)))
        p_ref[...] = phi

    pcut = p_ref[...]

    half = lax.broadcasted_iota(I32, (2 * HEAD_DIM, tq), 0) // HEAD_DIM
    for h in range(N_HEADS):
        pair = qT_ref[(h // 2) * 2 * HEAD_DIM:(h // 2 + 1) * 2 * HEAD_DIM, :]
        qz_ref[h] = jnp.where(half == (h % 2), pair, jnp.zeros_like(pair))
    m_ref[...] = jnp.full(m_ref.shape, NEG, F32)
    l_ref[...] = jnp.zeros(l_ref.shape, F32)
    acc_ref[...] = jnp.zeros(acc_ref.shape, F32)

    def qk(j, h):
        r0 = pl.multiple_of(j * ck, ck)
        kp = kn_ref[pl.ds(r0, ck), (h // 2) * 2 * HEAD_DIM:(h // 2 + 1) * 2 * HEAD_DIM]
        return jnp.dot(kp, qz_ref[h], preferred_element_type=F32)

    for h in range(LOOKAHEAD):
        s_ref[h] = qk(0, h)

    def attend(j, bias_w, has_next):
        r0 = pl.multiple_of(j * ck, ck)
        x = it_ref[pl.ds(r0, ck), :]
        sel = (x > tau) | ((x == tau) & (r0 + row_iota <= pcut))
        madd_ref[...] = jnp.where(sel, 0.0, NEG)
        ones = jnp.ones((2 * SUBLANES, ck), BF16)
        for h in range(N_HEADS):
            ahead = h + LOOKAHEAD
            if ahead < N_HEADS:
                s_ref[ahead] = qk(j, ahead)
            elif has_next:
                s_ref[ahead - N_HEADS] = qk(j + 1, ahead - N_HEADS)
            s = s_ref[h] + madd_ref[...]
            if bias_w is not None:
                s = s + bias_ref[bias_w, h]
            m_old = m_ref[h:h + 1, :]
            m_new = jnp.maximum(m_old, jnp.max(s, axis=0, keepdims=True))
            p = jnp.exp2(s - m_new)
            alpha = jnp.exp2(m_old - m_new)
            vt = vT_ref[h * HEAD_DIM:(h + 1) * HEAD_DIM, pl.ds(r0, ck)]
            pv = jnp.dot(jnp.concatenate([vt, ones], axis=0), p.astype(BF16),
                         preferred_element_type=F32)
            l_ref[h:h + 1, :] = alpha * l_ref[h:h + 1, :] + pv[HEAD_DIM:HEAD_DIM + 1, :]
            acc_ref[h * HEAD_DIM:(h + 1) * HEAD_DIM, :] = (
                alpha * acc_ref[h * HEAD_DIM:(h + 1) * HEAD_DIM, :] + pv[0:HEAD_DIM, :])
            m_ref[h:h + 1, :] = m_new

    def far_chunk(j, carry):
        attend(j, None, True)
        return carry

    lax.fori_loop(0, jnp.maximum(i - 1, 0), far_chunk, 0)

    @pl.when(i >= 1)
    def _():
        attend(i - 1, 1, True)

    attend(i, 0, False)

    for h in range(N_HEADS):
        sl = slice(h * HEAD_DIM, (h + 1) * HEAD_DIM)
        acc_ref[sl, :] = acc_ref[sl, :] / l_ref[h:h + 1, :]
    a_ref[...] = acc_ref[...].T.astype(BF16)


def _prompt_attention(qT, qiT, wiT, kn, kin, vTb, bias, tq, n_sel):
    batch, seq, _ = kn.shape
    nq = seq // tq
    blk = lambda b, i: (b, 0, i)
    per_b = lambda b, i: (b, 0, 0)
    return pl.pallas_call(
        functools.partial(_pattn_body, tq=tq, n_sel=n_sel),
        grid=(batch, nq),
        in_specs=[pl.BlockSpec((None, D_ATTN, tq), blk),
                  pl.BlockSpec((None, N_IDX_HEADS * IDX_DIM, tq), blk),
                  pl.BlockSpec((None, N_IDX_HEADS, tq), blk),
                  pl.BlockSpec((None, seq, D_ATTN), per_b),
                  pl.BlockSpec((None, seq, LANES), per_b),
                  pl.BlockSpec((None, D_ATTN, seq), per_b),
                  pl.BlockSpec(bias.shape, lambda b, i: (0, 0, 0, 0))],
        out_specs=pl.BlockSpec((None, tq, D_ATTN), lambda b, i: (b, i, 0)),
        out_shape=jax.ShapeDtypeStruct((batch, seq, D_ATTN), BF16),
        scratch_shapes=[pltpu.VMEM((seq, tq), F32),
                        pltpu.VMEM((seq + tq, tq), I16),
                        pltpu.VMEM((seq + tq, tq), I16),
                        pltpu.VMEM((N_HEADS, 2 * HEAD_DIM, tq), BF16),
                        pltpu.VMEM((N_HEADS, tq), F32),
                        pltpu.VMEM((N_HEADS, tq), F32),
                        pltpu.VMEM((D_ATTN, tq), F32),
                        pltpu.VMEM((1, tq), I32),
                        pltpu.VMEM((tq, tq), F32),
                        pltpu.VMEM((N_SBUF, tq, tq), F32)],
        compiler_params=pltpu.CompilerParams(dimension_semantics=("arbitrary", "arbitrary"),
                                             vmem_limit_bytes=VMEM_LIMIT),
        name="prompt_attention",
    )(qT, qiT, wiT, kn, kin, vTb, bias)


def _sidx_body(pt_ref, qi_ref, w_ref, kinew_ref, *rest, pg, n_pages, t_new, n_sel):
    pages = rest[:pg]
    madd_ref = rest[pg]
    it_ref, p_ref = rest[pg + 1:]
    g = pl.program_id(1)
    ps = pages[0].shape[-1]
    past = n_pages * ps
    tot = past + LANES
    n_slab = tot // LANES
    qi = qi_ref[...]
    w = w_ref[...]

    def scores(kt):
        s = jnp.dot(qi, kt.astype(BF16), preferred_element_type=F32)
        r = jnp.maximum(s, 0.0) * w
        return _tree_sum(r[h * t_new:(h + 1) * t_new] for h in range(N_IDX_HEADS))

    kt = jnp.concatenate([p[...] for p in pages], axis=-1)
    c0 = pl.multiple_of(g * (pg * ps), pg * ps)
    it_ref[:, pl.ds(c0, pg * ps)] = scores(kt)

    @pl.when(g == pl.num_programs(1) - 1)
    def _():
        sn = scores(kinew_ref[...].astype(F32))
        tq_i = lax.broadcasted_iota(I32, (t_new, LANES), 0)
        lane_i = lax.broadcasted_iota(I32, (t_new, LANES), 1)
        it_ref[:, past:tot] = jnp.where(lane_i <= tq_i, sn, -jnp.inf)

        def count(pred):
            parts = []
            for sl in range(n_slab):
                x = it_ref[:, sl * LANES:(sl + 1) * LANES]
                parts.append(jnp.where(pred(x, sl * LANES), 1.0, 0.0))
            return jnp.sum(_tree_sum(parts), axis=1, keepdims=True)

        def bis(_, st):
            lo, hi, c_lo, c_hi = st
            mid = _mid(lo, hi)
            midf = _key_to_float(mid)
            c = count(lambda x, c0_: x >= midf)
            ok = c >= n_sel
            return (jnp.where(ok, mid, lo), jnp.where(ok, hi, mid),
                    jnp.where(ok, c, c_lo), jnp.where(ok, c_hi, c))

        zero = jnp.zeros((t_new, 1), F32)
        lo, _, c_ge, c_gt = lax.fori_loop(
            0, N_BISECT, bis,
            (jnp.full((t_new, 1), KEY_LO, I32), jnp.full((t_new, 1), KEY_HI, I32), zero, zero))
        tau = _key_to_float(lo)
        need = n_sel - c_gt
        p_ref[...] = jnp.full((t_new, 1), tot, I32)

        @pl.when(jnp.max(c_ge) > n_sel)
        def _():
            def tie(_, lohi):
                plo, phi = lohi
                pm = (plo + phi) >> 1
                ok = count(lambda x, c0_: (x == tau) & (c0_ + lane_i <= pm)) >= need
                return jnp.where(ok, plo, pm), jnp.where(ok, pm, phi)
            n_it = int(math.ceil(math.log2(tot))) + 1
            _, phi = lax.fori_loop(0, n_it, tie, (jnp.full((t_new, 1), -1, I32),
                                                  jnp.full((t_new, 1), tot - 1, I32)))
            p_ref[...] = phi

        pcut = p_ref[...]
        for sl in range(n_slab):
            x = it_ref[:, sl * LANES:(sl + 1) * LANES]
            sel = (x > tau) | ((x == tau) & (sl * LANES + lane_i <= pcut))
            madd_ref[:, sl * LANES:(sl + 1) * LANES] = jnp.where(sel, 0.0, NEG)


def _sample_select(page_table, qi_s, w_s, kinew, kidxT, pg, n_sel):
    db, n_pages = page_table.shape
    ps = kidxT.shape[-1]
    t_new = qi_s.shape[1] // N_IDX_HEADS
    tot = n_pages * ps + LANES
    assert n_pages % pg == 0 and ps == LANES

    def page_spec(u):
        return pl.BlockSpec((None, IDX_DIM, ps), lambda b, g, pt: (pt[b, g * pg + u], 0, 0))

    grid_spec = pltpu.PrefetchScalarGridSpec(
        num_scalar_prefetch=1,
        grid=(db, n_pages // pg),
        in_specs=[pl.BlockSpec((None,) + qi_s.shape[1:], lambda b, g, pt: (b, 0, 0)),
                  pl.BlockSpec((None,) + w_s.shape[1:], lambda b, g, pt: (b, 0, 0)),
                  pl.BlockSpec((None,) + kinew.shape[1:], lambda b, g, pt: (b, 0, 0))]
                 + [page_spec(u) for u in range(pg)],
        out_specs=pl.BlockSpec((None, t_new, tot), lambda b, g, pt: (b, 0, 0)),
        scratch_shapes=[pltpu.VMEM((t_new, tot), F32), pltpu.VMEM((t_new, 1), I32)])
    return pl.pallas_call(
        functools.partial(_sidx_body, pg=pg, n_pages=n_pages, t_new=t_new, n_sel=n_sel),
        grid_spec=grid_spec,
        out_shape=jax.ShapeDtypeStruct((db, t_new, tot), F32),
        compiler_params=pltpu.CompilerParams(dimension_semantics=("arbitrary", "arbitrary"),
                                             vmem_limit_bytes=VMEM_LIMIT),
        name="sample_select",
    )(page_table, qi_s, w_s, kinew, *([kidxT] * pg))


def _sattn_body(pt_ref, qbd_ref, madd_ref, maddn_ref, knew_ref, vnew_ref, bias_ref, *rest, pg):
    kpages = rest[:pg]
    vpages = rest[pg:2 * pg]
    o_ref = rest[2 * pg]
    m_ref, l_ref, acc_ref = rest[2 * pg + 1:]
    g = pl.program_id(1)
    last = pl.num_programs(1) - 1
    qbd = qbd_ref[...]
    ps = kpages[0].shape[-1]
    t_new = madd_ref.shape[0]
    hd = N_HEADS * HEAD_DIM

    @pl.when(g == 0)
    def _():
        m_ref[...] = jnp.full(m_ref.shape, NEG, F32)
        l_ref[...] = jnp.zeros(l_ref.shape, F32)
        acc_ref[...] = jnp.zeros(acc_ref.shape, F32)

    def flash(s, vt):
        m_old = m_ref[...]
        m_new = jnp.maximum(m_old, jnp.max(s, axis=-1, keepdims=True))
        p = jnp.exp2(s - m_new)
        alpha = jnp.exp2(m_old - m_new)
        l_ref[...] = alpha * l_ref[...] + jnp.sum(p, axis=-1, keepdims=True)
        pv = lax.dot_general(p.astype(BF16), vt, (((1,), (1,)), ((), ())),
                             preferred_element_type=F32)
        acc_ref[...] = alpha * acc_ref[...] + pv
        m_ref[...] = m_new

    def add_rows(s, add):
        n = s.shape[-1]
        return (s.reshape(N_HEADS, t_new, n) + add[None]).reshape(N_HEADS * t_new, n)

    kt = jnp.concatenate([kp[...].reshape(hd, ps) for kp in kpages], axis=-1).astype(BF16)
    vt = jnp.concatenate([vp[...].reshape(hd, ps) for vp in vpages], axis=-1).astype(BF16)
    s = jnp.dot(qbd, kt, preferred_element_type=F32)
    s = add_rows(s, madd_ref[...])
    is_last = jnp.where(g == last, 1.0, 0.0)
    tail = s[:, (pg - 1) * ps:] + is_last * bias_ref[0].reshape(N_HEADS * t_new, ps)
    s = jnp.concatenate([s[:, :(pg - 1) * ps], tail], axis=-1)
    flash(s, vt)

    @pl.when(g == last)
    def _():
        sn = jnp.dot(qbd, knew_ref[...], preferred_element_type=F32)
        sn = add_rows(sn + bias_ref[1].reshape(N_HEADS * t_new, LANES), maddn_ref[...])
        flash(sn, vnew_ref[...])
        out = acc_ref[...] / l_ref[...]
        for h in range(N_HEADS):
            o_ref[h] = out[h * t_new:(h + 1) * t_new, h * HEAD_DIM:(h + 1) * HEAD_DIM]


def _sample_attention(page_table, qbd, madd, knew, vnew, bias_s, cache_kT, cache_vT, pg):
    db, n_pages = page_table.shape
    ps = cache_kT.shape[-1]
    t_new = madd.shape[1]
    assert n_pages % pg == 0

    def page_spec(u):
        return pl.BlockSpec((None, N_HEADS, HEAD_DIM, ps),
                            lambda b, g, pt: (pt[b, g * pg + u], 0, 0, 0))

    per_b = lambda b, g, pt: (b, 0, 0)
    grid_spec = pltpu.PrefetchScalarGridSpec(
        num_scalar_prefetch=1,
        grid=(db, n_pages // pg),
        in_specs=[pl.BlockSpec((None,) + qbd.shape[1:], per_b),
                  pl.BlockSpec((None, t_new, pg * ps), lambda b, g, pt: (b, 0, g)),
                  pl.BlockSpec((None, t_new, LANES), lambda b, g, pt: (b, 0, n_pages * ps // LANES)),
                  pl.BlockSpec((None,) + knew.shape[1:], per_b),
                  pl.BlockSpec((None,) + vnew.shape[1:], per_b),
                  pl.BlockSpec(bias_s.shape, lambda b, g, pt: (0, 0, 0, 0))]
                 + [page_spec(u) for u in range(pg)] * 2,
        out_specs=pl.BlockSpec((None, N_HEADS, t_new, HEAD_DIM), lambda b, g, pt: (b, 0, 0, 0)),
        scratch_shapes=[pltpu.VMEM((N_HEADS * t_new, 1), F32),
                        pltpu.VMEM((N_HEADS * t_new, 1), F32),
                        pltpu.VMEM((N_HEADS * t_new, N_HEADS * HEAD_DIM), F32)])
    return pl.pallas_call(
        functools.partial(_sattn_body, pg=pg),
        grid_spec=grid_spec,
        out_shape=jax.ShapeDtypeStruct((db, N_HEADS, t_new, HEAD_DIM), F32),
        compiler_params=pltpu.CompilerParams(dimension_semantics=("arbitrary", "arbitrary"),
                                             vmem_limit_bytes=VMEM_LIMIT),
        name="sample_attention",
    )(page_table, qbd, madd, madd, knew, vnew, bias_s, *([cache_kT] * pg), *([cache_vT] * pg))


def _outmlp_body(x_ref, a_ref, bg_ref, u_ref, prev_ref, cw_ref, wo_ref, gm_ref, wu_ref, wd_ref,
                 gf_ref, y_ref, *, seq_len, ff_chunk):
    tm = x_ref.shape[0]
    u = u_ref[...]
    w0 = cw_ref[0:1, :]
    w1 = cw_ref[1:2, :]
    w2 = cw_ref[2:3, :]
    if seq_len >= tm:
        first = (pl.program_id(0) % (seq_len // tm)) == 0
        halo = prev_ref[...] * jnp.where(first, 0.0, 1.0)
        row = lax.broadcasted_iota(I32, u.shape, 0)
        um1 = jnp.where(row == 0, halo[7:8, :], pltpu.roll(u, 1, 0))
        um2 = jnp.where(row == 0, halo[6:7, :],
                        jnp.where(row == 1, halo[7:8, :], pltpu.roll(u, 2, 0)))
    else:
        nseq = tm // seq_len
        u3 = u.reshape(nseq, seq_len, u.shape[-1])
        up = jnp.concatenate([prev_ref[...], u3], axis=1)
        um1 = up[:, 1:1 + seq_len].reshape(u.shape)
        um2 = up[:, 0:seq_len].reshape(u.shape)
    y = um2 * w0
    y = y + um1 * w1
    y = y + u * w2
    b = (bg_ref[...] * y).astype(BF16)
    ab = jnp.concatenate([a_ref[...], b], axis=-1)
    x1 = x_ref[...] + jnp.dot(ab, wo_ref[...], preferred_element_type=F32)
    ms = jnp.mean(x1 * x1, axis=-1, keepdims=True)
    hn = ((x1 * lax.rsqrt(ms + EPS)) * gm_ref[...]).astype(BF16)
    acc = jnp.zeros(x1.shape, F32)
    d_ff = wu_ref.shape[1]
    for c in range(d_ff // ff_chunk):
        sl = slice(c * ff_chunk, (c + 1) * ff_chunk)
        up_c = jnp.dot(hn, wu_ref[:, sl], preferred_element_type=F32)
        r = jnp.maximum(up_c, 0.0)
        acc = acc + jnp.dot((r * r).astype(BF16), wd_ref[sl, :], preferred_element_type=F32)
    x2 = x1 + acc
    ms2 = jnp.mean(x2 * x2, axis=-1, keepdims=True)
    y_ref[...] = (x2 * lax.rsqrt(ms2 + EPS)) * gf_ref[...]


def _out_mlp(x2d, a, bg, u, prev, conv_w, wo, g_mlp, wu, wd, g_final, tm, seq_len):
    n, d = x2d.shape
    dc = bg.shape[1]
    row = lambda i: (i, 0)
    c2 = lambda i: (0, 0)
    if seq_len >= tm:
        prev_spec = pl.BlockSpec((SUBLANES, dc),
                                 lambda i: (jnp.maximum(i * (tm // SUBLANES) - 1, 0), 0))
        prev_arg = u
    else:
        nseq = tm // seq_len
        prev_spec = pl.BlockSpec((nseq,) + prev.shape[1:], lambda i: (i, 0, 0))
        prev_arg = prev
    single = dict(pipeline_mode=pl.Buffered(1))
    return pl.pallas_call(
        functools.partial(_outmlp_body, seq_len=seq_len, ff_chunk=1024),
        grid=(n // tm,),
        in_specs=[pl.BlockSpec((tm, d), row),
                  pl.BlockSpec((tm, a.shape[1]), row),
                  pl.BlockSpec((tm, dc), row),
                  pl.BlockSpec((tm, dc), row),
                  prev_spec,
                  pl.BlockSpec(conv_w.shape, c2),
                  pl.BlockSpec(wo.shape, c2, **single),
                  pl.BlockSpec((1, d), c2),
                  pl.BlockSpec(wu.shape, c2, **single),
                  pl.BlockSpec(wd.shape, c2, **single),
                  pl.BlockSpec((1, d), c2)],
        out_specs=pl.BlockSpec((tm, d), row),
        out_shape=jax.ShapeDtypeStruct((n, d), F32),
        compiler_params=pltpu.CompilerParams(dimension_semantics=("arbitrary",),
                                             vmem_limit_bytes=VMEM_LIMIT),
        name="out_mlp",
    )(x2d, a, bg, u, prev_arg, conv_w, wo, g_mlp, wu, wd, g_final)


def _pick_tile(n, pref):
    t = min(pref, n)
    while n % t:
        t //= 2
    return t


def kernel(x_prompt, x_sample, cache_k, cache_v, cache_kidx, state_conv, page_table, rel_bias,
           g_mix, w_in, conv_w, w_out, g_mlp, w_up, w_down, g_final):
    depth = w_in.shape[0]
    assert depth == 1, "single-layer step"
    batch, seq, d_model = x_prompt.shape
    db, t_new, _ = x_sample.shape
    n_pages = page_table.shape[1]
    ps = cache_k.shape[2]
    past = n_pages * ps
    assert ps == LANES and t_new == SUBLANES

    tq = _pick_tile(seq, 256)
    n_sel_p = min(TOPK_MAX, seq // 4)
    n_sel_s = min(TOPK_MAX, (past + t_new) // 4)
    assert _far_bucket_is_constant(tq + 1, seq) and _far_bucket_is_constant(ps + 1, past + t_new)

    w = w_in[0]
    cq, ck_, cv, cqi, cki, cwi, cbg, ccg, ch = np.cumsum(
        [0, D_ATTN, D_ATTN, D_ATTN, N_IDX_HEADS * IDX_DIM, IDX_DIM, N_IDX_HEADS, D_ATTN, D_ATTN])
    end = ch + D_ATTN
    wt = jnp.pad(w[:, cq:cbg].T, ((0, T_END - cbg), (0, 0))).astype(BF16)
    wn = jnp.concatenate(
        [w[:, ck_:cv], w[:, cbg:end], jnp.pad(w[:, cki:cwi], ((0, 0), (0, LANES - IDX_DIM)))],
        axis=1).astype(BF16)
    wo = w_out[0].astype(BF16)
    wu = w_up[0].astype(BF16)
    wd = w_down[0].astype(BF16)
    gmix = g_mix[0][None]
    gmlp = g_mlp[0][None]
    gfin = g_final[None]
    cw = conv_w[0]

    n_p = batch * seq
    (qT, kT, vT, vTb, qiT, kiT, wiT, kn, kin, bg, u) = _in_proj(
        x_prompt, gmix, wn, wt, _pick_tile(seq, 512))
    bias_p = _bias_tables(rel_bias, tq, tq, (0, tq), -1, 1)
    a_p = _prompt_attention(qT, qiT, wiT, kn, kin, vTb, bias_p, tq, n_sel_p)
    y_prompt = _out_mlp(x_prompt.reshape(n_p, d_model), a_p.reshape(n_p, D_ATTN),
                        bg.reshape(n_p, D_ATTN), u.reshape(n_p, D_ATTN), None, cw, wo, gmlp, wu, wd,
                        gfin, _pick_tile(seq, 512), seq).reshape(batch, seq, d_model)

    def heads_out(t):
        b_, _, s_ = t.shape
        return t.reshape(b_, N_HEADS, HEAD_DIM, s_).transpose(0, 3, 1, 2)[None]

    k_prompt = heads_out(kT)
    v_prompt = heads_out(vT)
    kidx_prompt = kiT.transpose(0, 2, 1)[None]
    conv_prompt = u[:, seq - (CONV_WIDTH - 1):][None]

    ns = db * t_new
    (qTs, kTs, vTs, _, qiTs, kiTs, wiTs, _, _, bgs, us) = [
        t[0] for t in _in_proj(x_sample.reshape(1, ns, d_model), gmix, wn, wt, ns)]
    qi_s = qiTs.reshape(N_IDX_HEADS, IDX_DIM, db, t_new).transpose(2, 0, 3, 1).reshape(
        db, N_IDX_HEADS * t_new, IDX_DIM)
    w_s = wiTs.reshape(N_IDX_HEADS, db, t_new).transpose(1, 0, 2).reshape(db, N_IDX_HEADS * t_new, 1)
    kinew = jnp.pad(kiTs.reshape(IDX_DIM, db, t_new).transpose(1, 0, 2),
                    ((0, 0), (0, 0), (0, LANES - t_new))).astype(BF16)
    q_s = qTs.reshape(N_HEADS, HEAD_DIM, db, t_new).transpose(2, 0, 3, 1)
    eye = jnp.eye(N_HEADS, dtype=q_s.dtype)
    qbd = (q_s[:, :, :, None, :] * eye[None, :, None, :, None]).reshape(
        db, N_HEADS * t_new, N_HEADS * HEAD_DIM)
    pad_new = lambda t: jnp.pad(t.reshape(D_ATTN, db, t_new).transpose(1, 0, 2),
                                ((0, 0), (0, 0), (0, LANES - t_new))).astype(BF16)
    knew = pad_new(kTs)
    vnew = pad_new(vTs)
    kidxT = cache_kidx[0].transpose(0, 2, 1)
    cache_kT = cache_k[0].transpose(0, 2, 3, 1)
    cache_vT = cache_v[0].transpose(0, 2, 3, 1)

    madd = _sample_select(page_table, qi_s, w_s, kinew, kidxT, _pick_tile(n_pages, 32), n_sel_s)
    bias_s = _bias_tables(rel_bias, t_new, LANES, (ps, 0), 1, -1)
    o_s = _sample_attention(page_table, qbd, madd, knew, vnew, bias_s, cache_kT, cache_vT,
                            _pick_tile(n_pages, 16))
    a_s = o_s.transpose(0, 2, 1, 3).reshape(ns, D_ATTN).astype(BF16)
    y_sample = _out_mlp(x_sample.reshape(ns, d_model), a_s, bgs, us, state_conv[0], cw, wo, gmlp,
                        wu, wd, gfin, ns, t_new).reshape(db, t_new, d_model)

    def heads_out_s(t):
        return t.reshape(N_HEADS, HEAD_DIM, db, t_new).transpose(2, 3, 0, 1)[None]

    k_sample = heads_out_s(kTs)
    v_sample = heads_out_s(vTs)
    kidx_sample = kiTs.reshape(IDX_DIM, db, t_new).transpose(1, 2, 0)[None]
    conv_sample = us.reshape(db, t_new, D_ATTN)[:, t_new - (CONV_WIDTH - 1):][None]

    return (y_prompt, y_sample, k_prompt, v_prompt, kidx_prompt, conv_prompt,
            k_sample, v_sample, kidx_sample, conv_sample)
```

```python
import functools
import math

import jax
import jax.numpy as jnp
import numpy as np
from jax import lax
from jax.experimental import pallas as pl
from jax.experimental.pallas import tpu as pltpu

F32 = jnp.float32
BF16 = jnp.bfloat16
I32 = jnp.int32

HEAD_DIM = 64
N_HEADS = 8
N_IDX_HEADS = 8
IDX_DIM = 32
D_ATTN = N_HEADS * HEAD_DIM
TOPK_MAX = 256
CONV_WIDTH = 3
N_BUCKETS = 32
MAX_DISTANCE = 128
EPS = 1e-6
ATTN_SCALE = HEAD_DIM ** -0.5
INDEX_SCALE = (IDX_DIM ** -0.5) * (N_IDX_HEADS ** -0.5)
LOG2E = math.log2(math.e)

LANES = 128
SUBLANES = 8
NEG = -1e30
VMEM_LIMIT = 56 * 1024 * 1024


def _float_key(v):
    b = int(np.array(v, np.float32).view(np.int32))
    return b if b >= 0 else b ^ 0x7FFFFFFF


KEY_LO = _float_key(-np.finfo(np.float32).max)
KEY_HI = _float_key(np.inf)
N_BISECT = 32
assert KEY_HI - KEY_LO < 1 << N_BISECT
COUNT_ROWS = 64
N_SBUF = N_HEADS
LOOKAHEAD = N_HEADS - 1
assert N_IDX_HEADS == N_HEADS and 0 < LOOKAHEAD < N_SBUF


def _key_to_float(k):
    bits = k ^ ((k >> 31) & 0x7FFFFFFF)
    return lax.bitcast_convert_type(bits, F32)


def _mid(lo, hi):
    return (lo >> 1) + (hi >> 1) + (lo & hi & 1)


def _tree_sum(parts):
    parts = list(parts)
    while len(parts) > 1:
        nxt = [parts[k] + parts[k + 1] for k in range(0, len(parts) - 1, 2)]
        if len(parts) % 2:
            nxt.append(parts[-1])
        parts = nxt
    return parts[0]


def _bias_body(rb_ref, o_ref, *, offs, sa, sb):
    w = pl.program_id(0)
    h = pl.program_id(1)
    shape = o_ref.shape
    a = lax.broadcasted_iota(I32, shape, 0)
    b = lax.broadcasted_iota(I32, shape, 1)
    off = jnp.where(w == 0, offs[0], offs[1])
    dist = off + sa * a + sb * b
    n = jnp.maximum(dist, 0)
    max_exact = N_BUCKETS // 2
    nf = jnp.maximum(n, 1).astype(F32)
    large = max_exact + jnp.floor(jnp.log(nf / max_exact) / math.log(MAX_DISTANCE / max_exact)
                                  * (N_BUCKETS - max_exact)).astype(I32)
    large = jnp.minimum(large, N_BUCKETS - 1)
    bucket = jnp.where(n < max_exact, n, large)
    val = jnp.zeros(shape, F32)
    for k in range(N_BUCKETS):
        val = jnp.where(bucket == k, rb_ref[k, h], val)
    o_ref[...] = (val - rb_ref[N_BUCKETS - 1, h]) * LOG2E


def _bias_tables(rel_bias, rows, cols, offs, sa, sb):
    return pl.pallas_call(
        functools.partial(_bias_body, offs=offs, sa=sa, sb=sb),
        grid=(2, N_HEADS),
        in_specs=[pl.BlockSpec(memory_space=pltpu.SMEM)],
        out_specs=pl.BlockSpec((None, None, rows, cols), lambda w, h: (w, h, 0, 0)),
        out_shape=jax.ShapeDtypeStruct((2, N_HEADS, rows, cols), F32),
        name="bias_tables",
    )(rel_bias)


def _far_bucket_is_constant(min_dist, max_dist):
    d = np.arange(min_dist, max_dist + 1, dtype=np.float64)
    me = N_BUCKETS // 2
    b = me + np.floor(np.log(d / me) / math.log(MAX_DISTANCE / me) * (N_BUCKETS - me))
    return bool(np.all(np.minimum(b, N_BUCKETS - 1) == N_BUCKETS - 1)) and min_dist >= me


T_Q, T_K, T_V, T_QI, T_KI, T_WI, T_END = 0, 512, 1024, 1536, 1792, 1824, 1840
N_K, N_BG, N_CG, N_H, N_KI, N_END = 0, 512, 1024, 1536, 2048, 2176


def _inproj_body(x_ref, g_ref, wn_ref, wt_ref,
                 qT_ref, kT_ref, vT_ref, vTb_ref, qiT_ref, kiT_ref, wiT_ref,
                 kn_ref, kin_ref, bg_ref, u_ref):
    x = x_ref[...]
    ms = jnp.mean(x * x, axis=-1, keepdims=True)
    xn = ((x * lax.rsqrt(ms + EPS)) * g_ref[...]).astype(BF16)

    def nat(a, b):
        return jnp.dot(xn, wn_ref[:, a:b], preferred_element_type=F32)

    def tra(a, b):
        return lax.dot_general(wt_ref[a:b, :], xn, (((1,), (1,)), ((), ())),
                               preferred_element_type=F32)

    kn_ref[...] = nat(N_K, N_BG).astype(BF16)
    bg_ref[...] = nat(N_BG, N_CG)
    u_ref[...] = nat(N_CG, N_H) * nat(N_H, N_KI)
    kin_ref[...] = nat(N_KI, N_END).astype(BF16)

    qT_ref[...] = (tra(T_Q, T_K) * (ATTN_SCALE * LOG2E)).astype(BF16)
    kT_ref[...] = tra(T_K, T_V)
    vt = tra(T_V, T_QI)
    vT_ref[...] = vt
    vTb_ref[...] = vt.astype(BF16)
    qiT_ref[...] = tra(T_QI, T_KI).astype(BF16)
    kiT_ref[...] = tra(T_KI, T_WI)
    wiT_ref[...] = tra(T_WI, T_END)[0:N_IDX_HEADS, :] * INDEX_SCALE


def _in_proj(x3, g, wn, wt, tm):
    nb, seq, d = x3.shape
    assert seq % tm == 0
    const = lambda b, i: (0, 0)
    row = lambda b, i: (b, i, 0)
    col = lambda b, i: (b, 0, i)
    di = N_IDX_HEADS * IDX_DIM
    outs = [
        ((nb, D_ATTN, seq), BF16, (None, D_ATTN, tm), col),
        ((nb, D_ATTN, seq), F32, (None, D_ATTN, tm), col),
        ((nb, D_ATTN, seq), F32, (None, D_ATTN, tm), col),
        ((nb, D_ATTN, seq), BF16, (None, D_ATTN, tm), col),
        ((nb, di, seq), BF16, (None, di, tm), col),
        ((nb, IDX_DIM, seq), F32, (None, IDX_DIM, tm), col),
        ((nb, N_IDX_HEADS, seq), F32, (None, N_IDX_HEADS, tm), col),
        ((nb, seq, D_ATTN), BF16, (None, tm, D_ATTN), row),
        ((nb, seq, LANES), BF16, (None, tm, LANES), row),
        ((nb, seq, D_ATTN), F32, (None, tm, D_ATTN), row),
        ((nb, seq, D_ATTN), F32, (None, tm, D_ATTN), row),
    ]
    return pl.pallas_call(
        _inproj_body,
        grid=(nb, seq // tm),
        in_specs=[pl.BlockSpec((None, tm, d), row),
                  pl.BlockSpec((1, d), const),
                  pl.BlockSpec(wn.shape, const),
                  pl.BlockSpec(wt.shape, const)],
        out_specs=[pl.BlockSpec(bs, im) for (_, _, bs, im) in outs],
        out_shape=[jax.ShapeDtypeStruct(s, dt) for (s, dt, _, _) in outs],
        compiler_params=pltpu.CompilerParams(dimension_semantics=("arbitrary", "arbitrary"),
                                             vmem_limit_bytes=VMEM_LIMIT),
        name="in_proj",
    )(x3, g, wn, wt)


def _pattn_body(qT_ref, qiT_ref, wiT_ref, kn_ref, kin_ref, vT_ref, bias_ref, a_ref,
                it_ref, qz_ref, m_ref, l_ref, acc_ref, p_ref, s_ref,
                *, tq, n_sel):
    i = pl.program_id(1)
    nch = i + 1
    ck = tq
    s_tot = kn_ref.shape[0]
    q_idx = i * tq + lax.broadcasted_iota(I32, (ck, tq), 1)
    row_iota = lax.broadcasted_iota(I32, (ck, tq), 0)

    def idx_dot(j, h):
        r0 = pl.multiple_of(j * ck, ck)
        ki = kin_ref[pl.ds(r0, ck), :][:, 0:IDX_DIM]
        return jnp.dot(ki, qiT_ref[h * IDX_DIM:(h + 1) * IDX_DIM, :],
                       preferred_element_type=F32)

    for h in range(LOOKAHEAD):
        s_ref[h] = idx_dot(0, h)

    def idx_chunk(j, carry):
        r0 = pl.multiple_of(j * ck, ck)
        jn = jnp.minimum(j + 1, nch - 1)
        terms = []
        for h in range(N_IDX_HEADS):
            ahead = h + LOOKAHEAD
            if ahead < N_IDX_HEADS:
                s_ref[ahead] = idx_dot(j, ahead)
            else:
                s_ref[ahead - N_IDX_HEADS] = idx_dot(jn, ahead - N_IDX_HEADS)
            terms.append(wiT_ref[h:h + 1, :] * jnp.maximum(s_ref[h], 0.0))
        acc = _tree_sum(terms)
        acc = jnp.where(r0 + row_iota <= q_idx, acc, -jnp.inf)
        it_ref[pl.ds(r0, ck), :] = acc
        return carry

    lax.fori_loop(0, nch, idx_chunk, 0)

    ck2 = 2 * ck
    npair = (nch + 1) >> 1
    row_iota2 = lax.broadcasted_iota(I32, (COUNT_ROWS, tq), 0)

    @pl.when((nch & 1) == 1)
    def _():
        pad0 = pl.multiple_of(nch * ck, ck)
        it_ref[pl.ds(pad0, ck), :] = jnp.full((ck, tq), -jnp.inf, F32)

    def count(pred):
        def body(t, c8):
            r0 = pl.multiple_of(t * ck2, ck2)
            for sb in range(ck2 // COUNT_ROWS):
                base = r0 + sb * COUNT_ROWS
                x = it_ref[pl.ds(base, COUNT_ROWS), :]
                hit = jnp.where(pred(x, base), 1.0, 0.0)
                c8 = c8 + _tree_sum(hit[r * SUBLANES:(r + 1) * SUBLANES]
                                    for r in range(COUNT_ROWS // SUBLANES))
            return c8
        c8 = lax.fori_loop(0, npair, body, jnp.zeros((SUBLANES, tq), F32))
        return c8.sum(axis=0, keepdims=True)

    def bis(_, st):
        lo, hi, c_lo, c_hi = st
        mid = _mid(lo, hi)
        midf = _key_to_float(mid)
        c = count(lambda x, r0: x >= midf)
        ok = c >= n_sel
        return (jnp.where(ok, mid, lo), jnp.where(ok, hi, mid),
                jnp.where(ok, c, c_lo), jnp.where(ok, c_hi, c))

    zero = jnp.zeros((1, tq), F32)
    lo, _, c_ge, c_gt = lax.fori_loop(
        0, N_BISECT, bis,
        (jnp.full((1, tq), KEY_LO, I32), jnp.full((1, tq), KEY_HI, I32), zero, zero))
    tau = _key_to_float(lo)
    need = n_sel - c_gt

    p_ref[...] = jnp.full((1, tq), s_tot, I32)

    @pl.when(jnp.max(c_ge) > n_sel)
    def _():
        def tie(_, lohi):
            plo, phi = lohi
            pm = (plo + phi) >> 1
            ok = count(lambda x, r0: (x == tau) & (r0 + row_iota2 <= pm)) >= need
            return jnp.where(ok, plo, pm), jnp.where(ok, pm, phi)
        n_it = int(math.ceil(math.log2(s_tot))) + 1
        _, phi = lax.fori_loop(0, n_it, tie,
                               (jnp.full((1, tq), -1, I32), jnp.full((1, tq), s_tot - 1, I32)))
        p_ref[...] = phi

    pcut = p_ref[...]

    def mask_chunk(j, carry):
        r0 = pl.multiple_of(j * ck, ck)
        x = it_ref[pl.ds(r0, ck), :]
        sel = (x > tau) | ((x == tau) & (r0 + row_iota <= pcut))
        it_ref[pl.ds(r0, ck), :] = jnp.where(sel, 0.0, NEG)
        return carry

    lax.fori_loop(0, nch, mask_chunk, 0)

    half = lax.broadcasted_iota(I32, (2 * HEAD_DIM, tq), 0) // HEAD_DIM
    for h in range(N_HEADS):
        pair = qT_ref[(h // 2) * 2 * HEAD_DIM:(h // 2 + 1) * 2 * HEAD_DIM, :]
        qz_ref[h] = jnp.where(half == (h % 2), pair, jnp.zeros_like(pair))
    m_ref[...] = jnp.full(m_ref.shape, NEG, F32)
    l_ref[...] = jnp.zeros(l_ref.shape, F32)
    acc_ref[...] = jnp.zeros(acc_ref.shape, F32)

    def qk(j, h):
        r0 = pl.multiple_of(j * ck, ck)
        kp = kn_ref[pl.ds(r0, ck), (h // 2) * 2 * HEAD_DIM:(h // 2 + 1) * 2 * HEAD_DIM]
        return jnp.dot(kp, qz_ref[h], preferred_element_type=F32) + it_ref[pl.ds(r0, ck), :]

    for h in range(LOOKAHEAD):
        s_ref[h] = qk(0, h)

    def attend(j, bias_w, has_next):
        r0 = pl.multiple_of(j * ck, ck)
        ones = jnp.ones((2 * SUBLANES, ck), BF16)
        for h in range(N_HEADS):
            ahead = h + LOOKAHEAD
            if ahead < N_HEADS:
                s_ref[ahead] = qk(j, ahead)
            elif has_next:
                s_ref[ahead - N_HEADS] = qk(j + 1, ahead - N_HEADS)
            s = s_ref[h]
            if bias_w is not None:
                s = s + bias_ref[bias_w, h]
            m_old = m_ref[h:h + 1, :]
            m_new = jnp.maximum(m_old, jnp.max(s, axis=0, keepdims=True))
            p = jnp.exp2(s - m_new)
            alpha = jnp.exp2(m_old - m_new)
            vt = vT_ref[h * HEAD_DIM:(h + 1) * HEAD_DIM, pl.ds(r0, ck)]
            pv = jnp.dot(jnp.concatenate([vt, ones], axis=0), p.astype(BF16),
                         preferred_element_type=F32)
            l_ref[h:h + 1, :] = alpha * l_ref[h:h + 1, :] + pv[HEAD_DIM:HEAD_DIM + 1, :]
            acc_ref[h * HEAD_DIM:(h + 1) * HEAD_DIM, :] = (
                alpha * acc_ref[h * HEAD_DIM:(h + 1) * HEAD_DIM, :] + pv[0:HEAD_DIM, :])
            m_ref[h:h + 1, :] = m_new

    def far_chunk(j, carry):
        attend(j, None, True)
        return carry

    lax.fori_loop(0, jnp.maximum(i - 1, 0), far_chunk, 0)

    @pl.when(i >= 1)
    def _():
        attend(i - 1, 1, True)

    attend(i, 0, False)

    for h in range(N_HEADS):
        sl = slice(h * HEAD_DIM, (h + 1) * HEAD_DIM)
        acc_ref[sl, :] = acc_ref[sl, :] / l_ref[h:h + 1, :]
    a_ref[...] = acc_ref[...].T.astype(BF16)


def _prompt_attention(qT, qiT, wiT, kn, kin, vTb, bias, tq, n_sel):
    batch, seq, _ = kn.shape
    nq = seq // tq
    blk = lambda b, i: (b, 0, i)
    per_b = lambda b, i: (b, 0, 0)
    return pl.pallas_call(
        functools.partial(_pattn_body, tq=tq, n_sel=n_sel),
        grid=(batch, nq),
        in_specs=[pl.BlockSpec((None, D_ATTN, tq), blk),
                  pl.BlockSpec((None, N_IDX_HEADS * IDX_DIM, tq), blk),
                  pl.BlockSpec((None, N_IDX_HEADS, tq), blk),
                  pl.BlockSpec((None, seq, D_ATTN), per_b),
                  pl.BlockSpec((None, seq, LANES), per_b),
                  pl.BlockSpec((None, D_ATTN, seq), per_b),
                  pl.BlockSpec(bias.shape, lambda b, i: (0, 0, 0, 0))],
        out_specs=pl.BlockSpec((None, tq, D_ATTN), lambda b, i: (b, i, 0)),
        out_shape=jax.ShapeDtypeStruct((batch, seq, D_ATTN), BF16),
        scratch_shapes=[pltpu.VMEM((seq + tq, tq), F32),
                        pltpu.VMEM((N_HEADS, 2 * HEAD_DIM, tq), BF16),
                        pltpu.VMEM((N_HEADS, tq), F32),
                        pltpu.VMEM((N_HEADS, tq), F32),
                        pltpu.VMEM((D_ATTN, tq), F32),
                        pltpu.VMEM((1, tq), I32),
                        pltpu.VMEM((N_SBUF, tq, tq), F32)],
        compiler_params=pltpu.CompilerParams(dimension_semantics=("arbitrary", "arbitrary"),
                                             vmem_limit_bytes=VMEM_LIMIT),
        name="prompt_attention",
    )(qT, qiT, wiT, kn, kin, vTb, bias)


def _sidx_body(pt_ref, qi_ref, w_ref, kinew_ref, *rest, pg, n_pages, t_new, n_sel):
    pages = rest[:pg]
    madd_ref = rest[pg]
    it_ref, p_ref = rest[pg + 1:]
    g = pl.program_id(1)
    ps = pages[0].shape[-1]
    past = n_pages * ps
    tot = past + LANES
    n_slab = tot // LANES
    qi = qi_ref[...]
    w = w_ref[...]

    def scores(kt):
        s = jnp.dot(qi, kt.astype(BF16), preferred_element_type=F32)
        r = jnp.maximum(s, 0.0) * w
        return _tree_sum(r[h * t_new:(h + 1) * t_new] for h in range(N_IDX_HEADS))

    kt = jnp.concatenate([p[...] for p in pages], axis=-1)
    c0 = pl.multiple_of(g * (pg * ps), pg * ps)
    it_ref[:, pl.ds(c0, pg * ps)] = scores(kt)

    @pl.when(g == pl.num_programs(1) - 1)
    def _():
        sn = scores(kinew_ref[...].astype(F32))
        tq_i = lax.broadcasted_iota(I32, (t_new, LANES), 0)
        lane_i = lax.broadcasted_iota(I32, (t_new, LANES), 1)
        it_ref[:, past:tot] = jnp.where(lane_i <= tq_i, sn, -jnp.inf)

        def count(pred):
            parts = []
            for sl in range(n_slab):
                x = it_ref[:, sl * LANES:(sl + 1) * LANES]
                parts.append(jnp.where(pred(x, sl * LANES), 1.0, 0.0))
            return jnp.sum(_tree_sum(parts), axis=1, keepdims=True)

        def bis(_, st):
            lo, hi, c_lo, c_hi = st
            mid = _mid(lo, hi)
            midf = _key_to_float(mid)
            c = count(lambda x, c0_: x >= midf)
            ok = c >= n_sel
            return (jnp.where(ok, mid, lo), jnp.where(ok, hi, mid),
                    jnp.where(ok, c, c_lo), jnp.where(ok, c_hi, c))

        zero = jnp.zeros((t_new, 1), F32)
        lo, _, c_ge, c_gt = lax.fori_loop(
            0, N_BISECT, bis,
            (jnp.full((t_new, 1), KEY_LO, I32), jnp.full((t_new, 1), KEY_HI, I32), zero, zero))
        tau = _key_to_float(lo)
        need = n_sel - c_gt
        p_ref[...] = jnp.full((t_new, 1), tot, I32)

        @pl.when(jnp.max(c_ge) > n_sel)
        def _():
            def tie(_, lohi):
                plo, phi = lohi
                pm = (plo + phi) >> 1
                ok = count(lambda x, c0_: (x == tau) & (c0_ + lane_i <= pm)) >= need
                return jnp.where(ok, plo, pm), jnp.where(ok, pm, phi)
            n_it = int(math.ceil(math.log2(tot))) + 1
            _, phi = lax.fori_loop(0, n_it, tie, (jnp.full((t_new, 1), -1, I32),
                                                  jnp.full((t_new, 1), tot - 1, I32)))
            p_ref[...] = phi

        pcut = p_ref[...]
        for sl in range(n_slab):
            x = it_ref[:, sl * LANES:(sl + 1) * LANES]
            sel = (x > tau) | ((x == tau) & (sl * LANES + lane_i <= pcut))
            madd_ref[:, sl * LANES:(sl + 1) * LANES] = jnp.where(sel, 0.0, NEG)


def _sample_select(page_table, qi_s, w_s, kinew, kidxT, pg, n_sel):
    db, n_pages = page_table.shape
    ps = kidxT.shape[-1]
    t_new = qi_s.shape[1] // N_IDX_HEADS
    tot = n_pages * ps + LANES
    assert n_pages % pg == 0 and ps == LANES

    def page_spec(u):
        return pl.BlockSpec((None, IDX_DIM, ps), lambda b, g, pt: (pt[b, g * pg + u], 0, 0))

    grid_spec = pltpu.PrefetchScalarGridSpec(
        num_scalar_prefetch=1,
        grid=(db, n_pages // pg),
        in_specs=[pl.BlockSpec((None,) + qi_s.shape[1:], lambda b, g, pt: (b, 0, 0)),
                  pl.BlockSpec((None,) + w_s.shape[1:], lambda b, g, pt: (b, 0, 0)),
                  pl.BlockSpec((None,) + kinew.shape[1:], lambda b, g, pt: (b, 0, 0))]
                 + [page_spec(u) for u in range(pg)],
        out_specs=pl.BlockSpec((None, t_new, tot), lambda b, g, pt: (b, 0, 0)),
        scratch_shapes=[pltpu.VMEM((t_new, tot), F32), pltpu.VMEM((t_new, 1), I32)])
    return pl.pallas_call(
        functools.partial(_sidx_body, pg=pg, n_pages=n_pages, t_new=t_new, n_sel=n_sel),
        grid_spec=grid_spec,
        out_shape=jax.ShapeDtypeStruct((db, t_new, tot), F32),
        compiler_params=pltpu.CompilerParams(dimension_semantics=("arbitrary", "arbitrary"),
                                             vmem_limit_bytes=VMEM_LIMIT),
        name="sample_select",
    )(page_table, qi_s, w_s, kinew, *([kidxT] * pg))


def _sattn_body(pt_ref, qbd_ref, madd_ref, maddn_ref, knew_ref, vnew_ref, bias_ref, *rest, pg):
    kpages = rest[:pg]
    vpages = rest[pg:2 * pg]
    o_ref = rest[2 * pg]
    m_ref, l_ref, acc_ref = rest[2 * pg + 1:]
    g = pl.program_id(1)
    last = pl.num_programs(1) - 1
    qbd = qbd_ref[...]
    ps = kpages[0].shape[-1]
    t_new = madd_ref.shape[0]
    hd = N_HEADS * HEAD_DIM

    @pl.when(g == 0)
    def _():
        m_ref[...] = jnp.full(m_ref.shape, NEG, F32)
        l_ref[...] = jnp.zeros(l_ref.shape, F32)
        acc_ref[...] = jnp.zeros(acc_ref.shape, F32)

    def flash(s, vt):
        m_old = m_ref[...]
        m_new = jnp.maximum(m_old, jnp.max(s, axis=-1, keepdims=True))
        p = jnp.exp2(s - m_new)
        alpha = jnp.exp2(m_old - m_new)
        l_ref[...] = alpha * l_ref[...] + jnp.sum(p, axis=-1, keepdims=True)
        pv = lax.dot_general(p.astype(BF16), vt, (((1,), (1,)), ((), ())),
                             preferred_element_type=F32)
        acc_ref[...] = alpha * acc_ref[...] + pv
        m_ref[...] = m_new

    def add_rows(s, add):
        n = s.shape[-1]
        return (s.reshape(N_HEADS, t_new, n) + add[None]).reshape(N_HEADS * t_new, n)

    kt = jnp.concatenate([kp[...].reshape(hd, ps) for kp in kpages], axis=-1).astype(BF16)
    vt = jnp.concatenate([vp[...].reshape(hd, ps) for vp in vpages], axis=-1).astype(BF16)
    s = jnp.dot(qbd, kt, preferred_element_type=F32)
    s = add_rows(s, madd_ref[...])
    is_last = jnp.where(g == last, 1.0, 0.0)
    tail = s[:, (pg - 1) * ps:] + is_last * bias_ref[0].reshape(N_HEADS * t_new, ps)
    s = jnp.concatenate([s[:, :(pg - 1) * ps], tail], axis=-1)
    flash(s, vt)

    @pl.when(g == last)
    def _():
        sn = jnp.dot(qbd, knew_ref[...], preferred_element_type=F32)
        sn = add_rows(sn + bias_ref[1].reshape(N_HEADS * t_new, LANES), maddn_ref[...])
        flash(sn, vnew_ref[...])
        out = acc_ref[...] / l_ref[...]
        for h in range(N_HEADS):
            o_ref[h] = out[h * t_new:(h + 1) * t_new, h * HEAD_DIM:(h + 1) * HEAD_DIM]


def _sample_attention(page_table, qbd, madd, knew, vnew, bias_s, cache_kT, cache_vT, pg):
    db, n_pages = page_table.shape
    ps = cache_kT.shape[-1]
    t_new = madd.shape[1]
    assert n_pages % pg == 0

    def page_spec(u):
        return pl.BlockSpec((None, N_HEADS, HEAD_DIM, ps),
                            lambda b, g, pt: (pt[b, g * pg + u], 0, 0, 0))

    per_b = lambda b, g, pt: (b, 0, 0)
    grid_spec = pltpu.PrefetchScalarGridSpec(
        num_scalar_prefetch=1,
        grid=(db, n_pages // pg),
        in_specs=[pl.BlockSpec((None,) + qbd.shape[1:], per_b),
                  pl.BlockSpec((None, t_new, pg * ps), lambda b, g, pt: (b, 0, g)),
                  pl.BlockSpec((None, t_new, LANES), lambda b, g, pt: (b, 0, n_pages * ps // LANES)),
                  pl.BlockSpec((None,) + knew.shape[1:], per_b),
                  pl.BlockSpec((None,) + vnew.shape[1:], per_b),
                  pl.BlockSpec(bias_s.shape, lambda b, g, pt: (0, 0, 0, 0))]
                 + [page_spec(u) for u in range(pg)] * 2,
        out_specs=pl.BlockSpec((None, N_HEADS, t_new, HEAD_DIM), lambda b, g, pt: (b, 0, 0, 0)),
        scratch_shapes=[pltpu.VMEM((N_HEADS * t_new, 1), F32),
                        pltpu.VMEM((N_HEADS * t_new, 1), F32),
                        pltpu.VMEM((N_HEADS * t_new, N_HEADS * HEAD_DIM), F32)])
    return pl.pallas_call(
        functools.partial(_sattn_body, pg=pg),
        grid_spec=grid_spec,
        out_shape=jax.ShapeDtypeStruct((db, N_HEADS, t_new, HEAD_DIM), F32),
        compiler_params=pltpu.CompilerParams(dimension_semantics=("arbitrary", "arbitrary"),
                                             vmem_limit_bytes=VMEM_LIMIT),
        name="sample_attention",
    )(page_table, qbd, madd, madd, knew, vnew, bias_s, *([cache_kT] * pg), *([cache_vT] * pg))


def _outmlp_body(x_ref, a_ref, bg_ref, u_ref, prev_ref, cw_ref, wo_ref, gm_ref, wu_ref, wd_ref,
                 gf_ref, y_ref, *, seq_len, ff_chunk):
    tm = x_ref.shape[0]
    u = u_ref[...]
    w0 = cw_ref[0:1, :]
    w1 = cw_ref[1:2, :]
    w2 = cw_ref[2:3, :]
    if seq_len >= tm:
        first = (pl.program_id(0) % (seq_len // tm)) == 0
        halo = prev_ref[...] * jnp.where(first, 0.0, 1.0)
        row = lax.broadcasted_iota(I32, u.shape, 0)
        um1 = jnp.where(row == 0, halo[7:8, :], pltpu.roll(u, 1, 0))
        um2 = jnp.where(row == 0, halo[6:7, :],
                        jnp.where(row == 1, halo[7:8, :], pltpu.roll(u, 2, 0)))
    else:
        nseq = tm // seq_len
        u3 = u.reshape(nseq, seq_len, u.shape[-1])
        up = jnp.concatenate([prev_ref[...], u3], axis=1)
        um1 = up[:, 1:1 + seq_len].reshape(u.shape)
        um2 = up[:, 0:seq_len].reshape(u.shape)
    y = um2 * w0
    y = y + um1 * w1
    y = y + u * w2
    b = (bg_ref[...] * y).astype(BF16)
    ab = jnp.concatenate([a_ref[...], b], axis=-1)
    x1 = x_ref[...] + jnp.dot(ab, wo_ref[...], preferred_element_type=F32)
    ms = jnp.mean(x1 * x1, axis=-1, keepdims=True)
    hn = ((x1 * lax.rsqrt(ms + EPS)) * gm_ref[...]).astype(BF16)
    acc = jnp.zeros(x1.shape, F32)
    d_ff = wu_ref.shape[1]
    for c in range(d_ff // ff_chunk):
        sl = slice(c * ff_chunk, (c + 1) * ff_chunk)
        up_c = jnp.dot(hn, wu_ref[:, sl], preferred_element_type=F32)
        r = jnp.maximum(up_c, 0.0)
        acc = acc + jnp.dot((r * r).astype(BF16), wd_ref[sl, :], preferred_element_type=F32)
    x2 = x1 + acc
    ms2 = jnp.mean(x2 * x2, axis=-1, keepdims=True)
    y_ref[...] = (x2 * lax.rsqrt(ms2 + EPS)) * gf_ref[...]


def _out_mlp(x2d, a, bg, u, prev, conv_w, wo, g_mlp, wu, wd, g_final, tm, seq_len):
    n, d = x2d.shape
    dc = bg.shape[1]
    row = lambda i: (i, 0)
    c2 = lambda i: (0, 0)
    if seq_len >= tm:
        prev_spec = pl.BlockSpec((SUBLANES, dc),
                                 lambda i: (jnp.maximum(i * (tm // SUBLANES) - 1, 0), 0))
        prev_arg = u
    else:
        nseq = tm // seq_len
        prev_spec = pl.BlockSpec((nseq,) + prev.shape[1:], lambda i: (i, 0, 0))
        prev_arg = prev
    single = dict(pipeline_mode=pl.Buffered(1))
    return pl.pallas_call(
        functools.partial(_outmlp_body, seq_len=seq_len, ff_chunk=1024),
        grid=(n // tm,),
        in_specs=[pl.BlockSpec((tm, d), row),
                  pl.BlockSpec((tm, a.shape[1]), row),
                  pl.BlockSpec((tm, dc), row),
                  pl.BlockSpec((tm, dc), row),
                  prev_spec,
                  pl.BlockSpec(conv_w.shape, c2),
                  pl.BlockSpec(wo.shape, c2, **single),
                  pl.BlockSpec((1, d), c2),
                  pl.BlockSpec(wu.shape, c2, **single),
                  pl.BlockSpec(wd.shape, c2, **single),
                  pl.BlockSpec((1, d), c2)],
        out_specs=pl.BlockSpec((tm, d), row),
        out_shape=jax.ShapeDtypeStruct((n, d), F32),
        compiler_params=pltpu.CompilerParams(dimension_semantics=("arbitrary",),
                                             vmem_limit_bytes=VMEM_LIMIT),
        name="out_mlp",
    )(x2d, a, bg, u, prev_arg, conv_w, wo, g_mlp, wu, wd, g_final)


def _pick_tile(n, pref):
    t = min(pref, n)
    while n % t:
        t //= 2
    return t


def kernel(x_prompt, x_sample, cache_k, cache_v, cache_kidx, state_conv, page_table, rel_bias,
           g_mix, w_in, conv_w, w_out, g_mlp, w_up, w_down, g_final):
    depth = w_in.shape[0]
    assert depth == 1, "single-layer step"
    batch, seq, d_model = x_prompt.shape
    db, t_new, _ = x_sample.shape
    n_pages = page_table.shape[1]
    ps = cache_k.shape[2]
    past = n_pages * ps
    assert ps == LANES and t_new == SUBLANES

    tq = _pick_tile(seq, 256)
    n_sel_p = min(TOPK_MAX, seq // 4)
    n_sel_s = min(TOPK_MAX, (past + t_new) // 4)
    assert _far_bucket_is_constant(tq + 1, seq) and _far_bucket_is_constant(ps + 1, past + t_new)

    w = w_in[0]
    cq, ck_, cv, cqi, cki, cwi, cbg, ccg, ch = np.cumsum(
        [0, D_ATTN, D_ATTN, D_ATTN, N_IDX_HEADS * IDX_DIM, IDX_DIM, N_IDX_HEADS, D_ATTN, D_ATTN])
    end = ch + D_ATTN
    wt = jnp.pad(w[:, cq:cbg].T, ((0, T_END - cbg), (0, 0))).astype(BF16)
    wn = jnp.concatenate(
        [w[:, ck_:cv], w[:, cbg:end], jnp.pad(w[:, cki:cwi], ((0, 0), (0, LANES - IDX_DIM)))],
        axis=1).astype(BF16)
    wo = w_out[0].astype(BF16)
    wu = w_up[0].astype(BF16)
    wd = w_down[0].astype(BF16)
    gmix = g_mix[0][None]
    gmlp = g_mlp[0][None]
    gfin = g_final[None]
    cw = conv_w[0]

    n_p = batch * seq
    (qT, kT, vT, vTb, qiT, kiT, wiT, kn, kin, bg, u) = _in_proj(
        x_prompt, gmix, wn, wt, _pick_tile(seq, 512))
    bias_p = _bias_tables(rel_bias, tq, tq, (0, tq), -1, 1)
    a_p = _prompt_attention(qT, qiT, wiT, kn, kin, vTb, bias_p, tq, n_sel_p)
    y_prompt = _out_mlp(x_prompt.reshape(n_p, d_model), a_p.reshape(n_p, D_ATTN),
                        bg.reshape(n_p, D_ATTN), u.reshape(n_p, D_ATTN), None, cw, wo, gmlp, wu, wd,
                        gfin, _pick_tile(seq, 512), seq).reshape(batch, seq, d_model)

    def heads_out(t):
        b_, _, s_ = t.shape
        return t.reshape(b_, N_HEADS, HEAD_DIM, s_).transpose(0, 3, 1, 2)[None]

    k_prompt = heads_out(kT)
    v_prompt = heads_out(vT)
    kidx_prompt = kiT.transpose(0, 2, 1)[None]
    conv_prompt = u[:, seq - (CONV_WIDTH - 1):][None]

    ns = db * t_new
    (qTs, kTs, vTs, _, qiTs, kiTs, wiTs, _, _, bgs, us) = [
        t[0] for t in _in_proj(x_sample.reshape(1, ns, d_model), gmix, wn, wt, ns)]
    qi_s = qiTs.reshape(N_IDX_HEADS, IDX_DIM, db, t_new).transpose(2, 0, 3, 1).reshape(
        db, N_IDX_HEADS * t_new, IDX_DIM)
    w_s = wiTs.reshape(N_IDX_HEADS, db, t_new).transpose(1, 0, 2).reshape(db, N_IDX_HEADS * t_new, 1)
    kinew = jnp.pad(kiTs.reshape(IDX_DIM, db, t_new).transpose(1, 0, 2),
                    ((0, 0), (0, 0), (0, LANES - t_new))).astype(BF16)
    q_s = qTs.reshape(N_HEADS, HEAD_DIM, db, t_new).transpose(2, 0, 3, 1)
    eye = jnp.eye(N_HEADS, dtype=q_s.dtype)
    qbd = (q_s[:, :, :, None, :] * eye[None, :, None, :, None]).reshape(
        db, N_HEADS * t_new, N_HEADS * HEAD_DIM)
    pad_new = lambda t: jnp.pad(t.reshape(D_ATTN, db, t_new).transpose(1, 0, 2),
                                ((0, 0), (0, 0), (0, LANES - t_new))).astype(BF16)
    knew = pad_new(kTs)
    vnew = pad_new(vTs)
    kidxT = cache_kidx[0].transpose(0, 2, 1)
    cache_kT = cache_k[0].transpose(0, 2, 3, 1)
    cache_vT = cache_v[0].transpose(0, 2, 3, 1)

    madd = _sample_select(page_table, qi_s, w_s, kinew, kidxT, _pick_tile(n_pages, 32), n_sel_s)
    bias_s = _bias_tables(rel_bias, t_new, LANES, (ps, 0), 1, -1)
    o_s = _sample_attention(page_table, qbd, madd, knew, vnew, bias_s, cache_kT, cache_vT,
                            _pick_tile(n_pages, 16))
    a_s = o_s.transpose(0, 2, 1, 3).reshape(ns, D_ATTN).astype(BF16)
    y_sample = _out_mlp(x_sample.reshape(ns, d_model), a_s, bgs, us, state_conv[0], cw, wo, gmlp,
                        wu, wd, gfin, ns, t_new).reshape(db, t_new, d_model)

    def heads_out_s(t):
        return t.reshape(N_HEADS, HEAD_DIM, db, t_new).transpose(2, 3, 0, 1)[None]

    k_sample = heads_out_s(kTs)
    v_sample = heads_out_s(vTs)
    kidx_sample = kiTs.reshape(IDX_DIM, db, t_new).transpose(1, 2, 0)[None]
    conv_sample = us.reshape(db, t_new, D_ATTN)[:, t_new - (CONV_WIDTH - 1):][None]

    return (y_prompt, y_sample, k_prompt, v_prompt, kidx_prompt, conv_prompt,
            k_sample, v_sample, kidx_sample, conv_sample)
```

```python
import functools
import math

import jax
import jax.numpy as jnp
import numpy as np
from jax import lax
from jax.experimental import pallas as pl
from jax.experimental.pallas import tpu as pltpu

F32 = jnp.float32
BF16 = jnp.bfloat16
I32 = jnp.int32

HEAD_DIM = 64
N_HEADS = 8
N_IDX_HEADS = 8
IDX_DIM = 32
D_ATTN = N_HEADS * HEAD_DIM
TOPK_MAX = 256
CONV_WIDTH = 3
N_BUCKETS = 32
MAX_DISTANCE = 128
EPS = 1e-6
ATTN_SCALE = HEAD_DIM ** -0.5
INDEX_SCALE = (IDX_DIM ** -0.5) * (N_IDX_HEADS ** -0.5)
LOG2E = math.log2(math.e)

LANES = 128
SUBLANES = 8
NEG = -1e30
VMEM_LIMIT = 56 * 1024 * 1024


def _float_key(v):
    b = int(np.array(v, np.float32).view(np.int32))
    return b if b >= 0 else b ^ 0x7FFFFFFF


KEY_LO = _float_key(-np.finfo(np.float32).max)
KEY_HI = _float_key(np.inf)
N_BISECT = 32
assert KEY_HI - KEY_LO < 1 << N_BISECT
COUNT_ROWS = 64
N_SBUF = N_HEADS
LOOKAHEAD = N_HEADS - 1
assert N_IDX_HEADS == N_HEADS and 0 < LOOKAHEAD < N_SBUF


def _key_to_float(k):
    bits = k ^ ((k >> 31) & 0x7FFFFFFF)
    return lax.bitcast_convert_type(bits, F32)


def _mid(lo, hi):
    return (lo >> 1) + (hi >> 1) + (lo & hi & 1)


def _tree_sum(parts):
    parts = list(parts)
    while len(parts) > 1:
        nxt = [parts[k] + parts[k + 1] for k in range(0, len(parts) - 1, 2)]
        if len(parts) % 2:
            nxt.append(parts[-1])
        parts = nxt
    return parts[0]


def _bias_body(rb_ref, o_ref, *, offs, sa, sb):
    w = pl.program_id(0)
    h = pl.program_id(1)
    shape = o_ref.shape
    a = lax.broadcasted_iota(I32, shape, 0)
    b = lax.broadcasted_iota(I32, shape, 1)
    off = jnp.where(w == 0, offs[0], offs[1])
    dist = off + sa * a + sb * b
    n = jnp.maximum(dist, 0)
    max_exact = N_BUCKETS // 2
    nf = jnp.maximum(n, 1).astype(F32)
    large = max_exact + jnp.floor(jnp.log(nf / max_exact) / math.log(MAX_DISTANCE / max_exact)
                                  * (N_BUCKETS - max_exact)).astype(I32)
    large = jnp.minimum(large, N_BUCKETS - 1)
    bucket = jnp.where(n < max_exact, n, large)
    val = jnp.zeros(shape, F32)
    for k in range(N_BUCKETS):
        val = jnp.where(bucket == k, rb_ref[k, h], val)
    o_ref[...] = (val - rb_ref[N_BUCKETS - 1, h]) * LOG2E


def _bias_tables(rel_bias, rows, cols, offs, sa, sb):
    return pl.pallas_call(
        functools.partial(_bias_body, offs=offs, sa=sa, sb=sb),
        grid=(2, N_HEADS),
        in_specs=[pl.BlockSpec(memory_space=pltpu.SMEM)],
        out_specs=pl.BlockSpec((None, None, rows, cols), lambda w, h: (w, h, 0, 0)),
        out_shape=jax.ShapeDtypeStruct((2, N_HEADS, rows, cols), F32),
        name="bias_tables",
    )(rel_bias)


def _far_bucket_is_constant(min_dist, max_dist):
    d = np.arange(min_dist, max_dist + 1, dtype=np.float64)
    me = N_BUCKETS // 2
    b = me + np.floor(np.log(d / me) / math.log(MAX_DISTANCE / me) * (N_BUCKETS - me))
    return bool(np.all(np.minimum(b, N_BUCKETS - 1) == N_BUCKETS - 1)) and min_dist >= me


T_Q, T_K, T_V, T_QI, T_KI, T_WI, T_END = 0, 512, 1024, 1536, 1792, 1824, 1840
N_K, N_BG, N_CG, N_H, N_KI, N_END = 0, 512, 1024, 1536, 2048, 2176


def _inproj_body(x_ref, g_ref, wn_ref, wt_ref,
                 qT_ref, kT_ref, vT_ref, vTb_ref, qiT_ref, kiT_ref, wiT_ref,
                 kn_ref, kin_ref, bg_ref, u_ref):
    x = x_ref[...]
    ms = jnp.mean(x * x, axis=-1, keepdims=True)
    xn = ((x * lax.rsqrt(ms + EPS)) * g_ref[...]).astype(BF16)

    def nat(a, b):
        return jnp.dot(xn, wn_ref[:, a:b], preferred_element_type=F32)

    def tra(a, b):
        return lax.dot_general(wt_ref[a:b, :], xn, (((1,), (1,)), ((), ())),
                               preferred_element_type=F32)

    kn_ref[...] = nat(N_K, N_BG).astype(BF16)
    bg_ref[...] = nat(N_BG, N_CG)
    u_ref[...] = nat(N_CG, N_H) * nat(N_H, N_KI)
    kin_ref[...] = nat(N_KI, N_END).astype(BF16)

    qT_ref[...] = (tra(T_Q, T_K) * (ATTN_SCALE * LOG2E)).astype(BF16)
    kT_ref[...] = tra(T_K, T_V)
    vt = tra(T_V, T_QI)
    vT_ref[...] = vt
    vTb_ref[...] = vt.astype(BF16)
    qiT_ref[...] = tra(T_QI, T_KI).astype(BF16)
    kiT_ref[...] = tra(T_KI, T_WI)
    wiT_ref[...] = tra(T_WI, T_END)[0:N_IDX_HEADS, :] * INDEX_SCALE


def _in_proj(x3, g, wn, wt, tm):
    nb, seq, d = x3.shape
    assert seq % tm == 0
    const = lambda b, i: (0, 0)
    row = lambda b, i: (b, i, 0)
    col = lambda b, i: (b, 0, i)
    di = N_IDX_HEADS * IDX_DIM
    outs = [
        ((nb, D_ATTN, seq), BF16, (None, D_ATTN, tm), col),
        ((nb, D_ATTN, seq), F32, (None, D_ATTN, tm), col),
        ((nb, D_ATTN, seq), F32, (None, D_ATTN, tm), col),
        ((nb, D_ATTN, seq), BF16, (None, D_ATTN, tm), col),
        ((nb, di, seq), BF16, (None, di, tm), col),
        ((nb, IDX_DIM, seq), F32, (None, IDX_DIM, tm), col),
        ((nb, N_IDX_HEADS, seq), F32, (None, N_IDX_HEADS, tm), col),
        ((nb, seq, D_ATTN), BF16, (None, tm, D_ATTN), row),
        ((nb, seq, LANES), BF16, (None, tm, LANES), row),
        ((nb, seq, D_ATTN), F32, (None, tm, D_ATTN), row),
        ((nb, seq, D_ATTN), F32, (None, tm, D_ATTN), row),
    ]
    return pl.pallas_call(
        _inproj_body,
        grid=(nb, seq // tm),
        in_specs=[pl.BlockSpec((None, tm, d), row),
                  pl.BlockSpec((1, d), const),
                  pl.BlockSpec(wn.shape, const),
                  pl.BlockSpec(wt.shape, const)],
        out_specs=[pl.BlockSpec(bs, im) for (_, _, bs, im) in outs],
        out_shape=[jax.ShapeDtypeStruct(s, dt) for (s, dt, _, _) in outs],
        compiler_params=pltpu.CompilerParams(dimension_semantics=("arbitrary", "arbitrary"),
                                             vmem_limit_bytes=VMEM_LIMIT),
        name="in_proj",
    )(x3, g, wn, wt)


def _pattn_body(qT_ref, qiT_ref, wiT_ref, kn_ref, kin_ref, vT_ref, bias_ref, a_ref,
                it_ref, qz_ref, m_ref, l_ref, acc_ref, p_ref, s_ref,
                *, tq, n_sel):
    i = pl.program_id(1)
    nch = i + 1
    ck = tq
    s_tot = kn_ref.shape[0]
    q_idx = i * tq + lax.broadcasted_iota(I32, (ck, tq), 1)
    row_iota = lax.broadcasted_iota(I32, (ck, tq), 0)

    def idx_dot(j, h):
        r0 = pl.multiple_of(j * ck, ck)
        ki = kin_ref[pl.ds(r0, ck), :][:, 0:IDX_DIM]
        return jnp.dot(ki, qiT_ref[h * IDX_DIM:(h + 1) * IDX_DIM, :],
                       preferred_element_type=F32)

    for h in range(LOOKAHEAD):
        s_ref[h] = idx_dot(0, h)

    def idx_chunk(j, carry):
        r0 = pl.multiple_of(j * ck, ck)
        jn = jnp.minimum(j + 1, nch - 1)
        terms = []
        for h in range(N_IDX_HEADS):
            ahead = h + LOOKAHEAD
            if ahead < N_IDX_HEADS:
                s_ref[ahead] = idx_dot(j, ahead)
            else:
                s_ref[ahead - N_IDX_HEADS] = idx_dot(jn, ahead - N_IDX_HEADS)
            terms.append(wiT_ref[h:h + 1, :] * jnp.maximum(s_ref[h], 0.0))
        acc = _tree_sum(terms)
        acc = jnp.where(r0 + row_iota <= q_idx, acc, -jnp.inf)
        it_ref[pl.ds(r0, ck), :] = acc
        return carry

    lax.fori_loop(0, nch, idx_chunk, 0)

    ck2 = 2 * ck
    npair = (nch + 1) >> 1
    row_iota2 = lax.broadcasted_iota(I32, (COUNT_ROWS, tq), 0)

    @pl.when((nch & 1) == 1)
    def _():
        pad0 = pl.multiple_of(nch * ck, ck)
        it_ref[pl.ds(pad0, ck), :] = jnp.full((ck, tq), -jnp.inf, F32)

    def count(pred):
        def body(t, c8):
            r0 = pl.multiple_of(t * ck2, ck2)
            for sb in range(ck2 // COUNT_ROWS):
                base = r0 + sb * COUNT_ROWS
                x = it_ref[pl.ds(base, COUNT_ROWS), :]
                hit = jnp.where(pred(x, base), 1.0, 0.0)
                c8 = c8 + _tree_sum(hit[r * SUBLANES:(r + 1) * SUBLANES]
                                    for r in range(COUNT_ROWS // SUBLANES))
            return c8
        c8 = lax.fori_loop(0, npair, body, jnp.zeros((SUBLANES, tq), F32))
        return c8.sum(axis=0, keepdims=True)

    def bis(_, st):
        lo, hi, c_lo, c_hi = st
        mid = _mid(lo, hi)
        midf = _key_to_float(mid)
        c = count(lambda x, r0: x >= midf)
        ok = c >= n_sel
        return (jnp.where(ok, mid, lo), jnp.where(ok, hi, mid),
                jnp.where(ok, c, c_lo), jnp.where(ok, c_hi, c))

    zero = jnp.zeros((1, tq), F32)
    lo, _, c_ge, c_gt = lax.fori_loop(
        0, N_BISECT, bis,
        (jnp.full((1, tq), KEY_LO, I32), jnp.full((1, tq), KEY_HI, I32), zero, zero))
    tau = _key_to_float(lo)
    need = n_sel - c_gt

    p_ref[...] = jnp.full((1, tq), s_tot, I32)

    @pl.when(jnp.max(c_ge) > n_sel)
    def _():
        def tie(_, lohi):
            plo, phi = lohi
            pm = (plo + phi) >> 1
            ok = count(lambda x, r0: (x == tau) & (r0 + row_iota2 <= pm)) >= need
            return jnp.where(ok, plo, pm), jnp.where(ok, pm, phi)
        n_it = int(math.ceil(math.log2(s_tot))) + 1
        _, phi = lax.fori_loop(0, n_it, tie,
                               (jnp.full((1, tq), -1, I32), jnp.full((1, tq), s_tot - 1, I32)))
        p_ref[...] = phi

    pcut = p_ref[...]

    def mask_chunk(j, carry):
        r0 = pl.multiple_of(j * ck, ck)
        x = it_ref[pl.ds(r0, ck), :]
        sel = (x > tau) | ((x == tau) & (r0 + row_iota <= pcut))
        it_ref[pl.ds(r0, ck), :] = jnp.where(sel, 0.0, NEG)
        return carry

    lax.fori_loop(0, nch, mask_chunk, 0)

    half = lax.broadcasted_iota(I32, (2 * HEAD_DIM, tq), 0) // HEAD_DIM
    for h in range(N_HEADS):
        pair = qT_ref[(h // 2) * 2 * HEAD_DIM:(h // 2 + 1) * 2 * HEAD_DIM, :]
        qz_ref[h] = jnp.where(half == (h % 2), pair, jnp.zeros_like(pair))
    m_ref[...] = jnp.full(m_ref.shape, NEG, F32)
    l_ref[...] = jnp.zeros(l_ref.shape, F32)
    acc_ref[...] = jnp.zeros(acc_ref.shape, F32)

    def qk(j, h):
        r0 = pl.multiple_of(j * ck, ck)
        kp = kn_ref[pl.ds(r0, ck), (h // 2) * 2 * HEAD_DIM:(h // 2 + 1) * 2 * HEAD_DIM]
        return jnp.dot(kp, qz_ref[h], preferred_element_type=F32) + it_ref[pl.ds(r0, ck), :]

    for h in range(LOOKAHEAD):
        s_ref[h] = qk(0, h)

    def attend(j, bias_w, has_next):
        r0 = pl.multiple_of(j * ck, ck)
        ones = jnp.ones((2 * SUBLANES, ck), BF16)
        for h in range(N_HEADS):
            ahead = h + LOOKAHEAD
            if ahead < N_HEADS:
                s_ref[ahead] = qk(j, ahead)
            elif has_next:
                s_ref[ahead - N_HEADS] = qk(j + 1, ahead - N_HEADS)
            s = s_ref[h]
            if bias_w is not None:
                s = s + bias_ref[bias_w, h]
            m_old = m_ref[h:h + 1, :]
            m_new = jnp.maximum(m_old, jnp.max(s, axis=0, keepdims=True))
            p = jnp.exp2(s - m_new)
            alpha = jnp.exp2(m_old - m_new)
            vt = vT_ref[h * HEAD_DIM:(h + 1) * HEAD_DIM, pl.ds(r0, ck)]
            pv = jnp.dot(jnp.concatenate([vt, ones], axis=0), p.astype(BF16),
                         preferred_element_type=F32)
            l_ref[h:h + 1, :] = alpha * l_ref[h:h + 1, :] + pv[HEAD_DIM:HEAD_DIM + 1, :]
            acc_ref[h * HEAD_DIM:(h + 1) * HEAD_DIM, :] = (
                alpha * acc_ref[h * HEAD_DIM:(h + 1) * HEAD_DIM, :] + pv[0:HEAD_DIM, :])
            m_ref[h:h + 1, :] = m_new

    def far_chunk(j, carry):
        attend(j, None, True)
        return carry

    lax.fori_loop(0, jnp.maximum(i - 1, 0), far_chunk, 0)

    @pl.when(i >= 1)
    def _():
        attend(i - 1, 1, True)

    attend(i, 0, False)

    for h in range(N_HEADS):
        sl = slice(h * HEAD_DIM, (h + 1) * HEAD_DIM)
        acc_ref[sl, :] = acc_ref[sl, :] / l_ref[h:h + 1, :]
    a_ref[...] = acc_ref[...].T.astype(BF16)


def _prompt_attention(qT, qiT, wiT, kn, kin, vTb, bias, tq, n_sel):
    batch, seq, _ = kn.shape
    nq = seq // tq
    blk = lambda b, i: (b, 0, i)
    per_b = lambda b, i: (b, 0, 0)
    return pl.pallas_call(
        functools.partial(_pattn_body, tq=tq, n_sel=n_sel),
        grid=(batch, nq),
        in_specs=[pl.BlockSpec((None, D_ATTN, tq), blk),
                  pl.BlockSpec((None, N_IDX_HEADS * IDX_DIM, tq), blk),
                  pl.BlockSpec((None, N_IDX_HEADS, tq), blk),
                  pl.BlockSpec((None, seq, D_ATTN), per_b),
                  pl.BlockSpec((None, seq, LANES), per_b),
                  pl.BlockSpec((None, D_ATTN, seq), per_b),
                  pl.BlockSpec(bias.shape, lambda b, i: (0, 0, 0, 0))],
        out_specs=pl.BlockSpec((None, tq, D_ATTN), lambda b, i: (b, i, 0)),
        out_shape=jax.ShapeDtypeStruct((batch, seq, D_ATTN), BF16),
        scratch_shapes=[pltpu.VMEM((seq + tq, tq), F32),
                        pltpu.VMEM((N_HEADS, 2 * HEAD_DIM, tq), BF16),
                        pltpu.VMEM((N_HEADS, tq), F32),
                        pltpu.VMEM((N_HEADS, tq), F32),
                        pltpu.VMEM((D_ATTN, tq), F32),
                        pltpu.VMEM((1, tq), I32),
                        pltpu.VMEM((N_SBUF, tq, tq), F32)],
        compiler_params=pltpu.CompilerParams(dimension_semantics=("arbitrary", "arbitrary"),
                                             vmem_limit_bytes=VMEM_LIMIT),
        name="prompt_attention",
    )(qT, qiT, wiT, kn, kin, vTb, bias)


def _sidx_body(pt_ref, qi_ref, w_ref, kinew_ref, *rest, pg, n_pages, t_new, n_sel, group):
    pages = rest[:pg]
    madd_ref = rest[pg]
    it_ref, p_ref = rest[pg + 1:]
    b = pl.program_id(0)
    g = pl.program_id(1)
    ps = pages[0].shape[-1]
    past = n_pages * ps
    tot = past + LANES
    n_slab = tot // LANES
    rows = group * t_new
    rb = pl.multiple_of((b % group) * t_new, t_new)
    qi = qi_ref[...]
    w = w_ref[...]

    def scores(kt):
        s = jnp.dot(qi, kt.astype(BF16), preferred_element_type=F32)
        r = jnp.maximum(s, 0.0) * w
        return _tree_sum(r[h * t_new:(h + 1) * t_new] for h in range(N_IDX_HEADS))

    kt = jnp.concatenate([p[...] for p in pages], axis=-1)
    c0 = pl.multiple_of(g * (pg * ps), pg * ps)
    it_ref[pl.ds(rb, t_new), pl.ds(c0, pg * ps)] = scores(kt)
    last_g = g == pl.num_programs(1) - 1

    @pl.when(last_g)
    def _():
        sn = scores(kinew_ref[...].astype(F32))
        tq_i = lax.broadcasted_iota(I32, (t_new, LANES), 0)
        tk_i = lax.broadcasted_iota(I32, (t_new, LANES), 1)
        it_ref[pl.ds(rb, t_new), past:tot] = jnp.where(tk_i <= tq_i, sn, -jnp.inf)

    @pl.when(last_g & (b % group == group - 1))
    def _():
        lane_i = lax.broadcasted_iota(I32, (rows, LANES), 1)

        def count(pred):
            accs = [jnp.zeros((rows, LANES), F32) for _ in range(2)]
            for sl in range(n_slab):
                x = it_ref[:, sl * LANES:(sl + 1) * LANES]
                accs[sl % 2] = accs[sl % 2] + jnp.where(pred(x, sl * LANES), 1.0, 0.0)
            return jnp.sum(accs[0] + accs[1], axis=1, keepdims=True)

        def bis(_, st):
            lo, hi, c_lo, c_hi = st
            mid = _mid(lo, hi)
            midf = _key_to_float(mid)
            c = count(lambda x, c0_: x >= midf)
            ok = c >= n_sel
            return (jnp.where(ok, mid, lo), jnp.where(ok, hi, mid),
                    jnp.where(ok, c, c_lo), jnp.where(ok, c_hi, c))

        zero = jnp.zeros((rows, 1), F32)
        lo, _, c_ge, c_gt = lax.fori_loop(
            0, N_BISECT, bis,
            (jnp.full((rows, 1), KEY_LO, I32), jnp.full((rows, 1), KEY_HI, I32), zero, zero))
        tau = _key_to_float(lo)
        need = n_sel - c_gt
        p_ref[...] = jnp.full((rows, 1), tot, I32)

        @pl.when(jnp.max(c_ge) > n_sel)
        def _():
            def tie(_, lohi):
                plo, phi = lohi
                pm = (plo + phi) >> 1
                ok = count(lambda x, c0_: (x == tau) & (c0_ + lane_i <= pm)) >= need
                return jnp.where(ok, plo, pm), jnp.where(ok, pm, phi)
            n_it = int(math.ceil(math.log2(tot))) + 1
            _, phi = lax.fori_loop(0, n_it, tie, (jnp.full((rows, 1), -1, I32),
                                                  jnp.full((rows, 1), tot - 1, I32)))
            p_ref[...] = phi

        pcut = p_ref[...]
        for sl in range(n_slab):
            x = it_ref[:, sl * LANES:(sl + 1) * LANES]
            sel = (x > tau) | ((x == tau) & (sl * LANES + lane_i <= pcut))
            madd_ref[:, :, sl * LANES:(sl + 1) * LANES] = jnp.where(sel, 0.0, NEG).reshape(
                group, t_new, LANES)


def _sample_select(page_table, qi_s, w_s, kinew, kidxT, pg, n_sel, group):
    db, n_pages = page_table.shape
    ps = kidxT.shape[-1]
    t_new = qi_s.shape[1] // N_IDX_HEADS
    tot = n_pages * ps + LANES
    assert n_pages % pg == 0 and ps == LANES and db % group == 0

    def page_spec(u):
        return pl.BlockSpec((None, IDX_DIM, ps), lambda b, g, pt: (pt[b, g * pg + u], 0, 0))

    grid_spec = pltpu.PrefetchScalarGridSpec(
        num_scalar_prefetch=1,
        grid=(db, n_pages // pg),
        in_specs=[pl.BlockSpec((None,) + qi_s.shape[1:], lambda b, g, pt: (b, 0, 0)),
                  pl.BlockSpec((None,) + w_s.shape[1:], lambda b, g, pt: (b, 0, 0)),
                  pl.BlockSpec((None,) + kinew.shape[1:], lambda b, g, pt: (b, 0, 0))]
                 + [page_spec(u) for u in range(pg)],
        out_specs=pl.BlockSpec((group, t_new, tot), lambda b, g, pt: (b // group, 0, 0)),
        scratch_shapes=[pltpu.VMEM((group * t_new, tot), F32),
                        pltpu.VMEM((group * t_new, 1), I32)])
    return pl.pallas_call(
        functools.partial(_sidx_body, pg=pg, n_pages=n_pages, t_new=t_new, n_sel=n_sel,
                          group=group),
        grid_spec=grid_spec,
        out_shape=jax.ShapeDtypeStruct((db, t_new, tot), F32),
        compiler_params=pltpu.CompilerParams(dimension_semantics=("arbitrary", "arbitrary"),
                                             vmem_limit_bytes=VMEM_LIMIT),
        name="sample_select",
    )(page_table, qi_s, w_s, kinew, *([kidxT] * pg))


def _sattn_body(pt_ref, qbd_ref, madd_ref, maddn_ref, knew_ref, vnew_ref, bias_ref, *rest, pg):
    kpages = rest[:pg]
    vpages = rest[pg:2 * pg]
    o_ref = rest[2 * pg]
    m_ref, l_ref, acc_ref = rest[2 * pg + 1:]
    g = pl.program_id(1)
    last = pl.num_programs(1) - 1
    qbd = qbd_ref[...]
    ps = kpages[0].shape[-1]
    t_new = madd_ref.shape[0]
    hd = N_HEADS * HEAD_DIM

    @pl.when(g == 0)
    def _():
        m_ref[...] = jnp.full(m_ref.shape, NEG, F32)
        l_ref[...] = jnp.zeros(l_ref.shape, F32)
        acc_ref[...] = jnp.zeros(acc_ref.shape, F32)

    def flash(s, vt):
        m_old = m_ref[...]
        m_new = jnp.maximum(m_old, jnp.max(s, axis=-1, keepdims=True))
        p = jnp.exp2(s - m_new)
        alpha = jnp.exp2(m_old - m_new)
        l_ref[...] = alpha * l_ref[...] + jnp.sum(p, axis=-1, keepdims=True)
        pv = lax.dot_general(p.astype(BF16), vt, (((1,), (1,)), ((), ())),
                             preferred_element_type=F32)
        acc_ref[...] = alpha * acc_ref[...] + pv
        m_ref[...] = m_new

    def add_rows(s, add):
        n = s.shape[-1]
        return (s.reshape(N_HEADS, t_new, n) + add[None]).reshape(N_HEADS * t_new, n)

    kt = jnp.concatenate([kp[...].reshape(hd, ps) for kp in kpages], axis=-1).astype(BF16)
    vt = jnp.concatenate([vp[...].reshape(hd, ps) for vp in vpages], axis=-1).astype(BF16)
    s = jnp.dot(qbd, kt, preferred_element_type=F32)
    s = add_rows(s, madd_ref[...])
    is_last = jnp.where(g == last, 1.0, 0.0)
    tail = s[:, (pg - 1) * ps:] + is_last * bias_ref[0].reshape(N_HEADS * t_new, ps)
    s = jnp.concatenate([s[:, :(pg - 1) * ps], tail], axis=-1)
    flash(s, vt)

    @pl.when(g == last)
    def _():
        sn = jnp.dot(qbd, knew_ref[...], preferred_element_type=F32)
        sn = add_rows(sn + bias_ref[1].reshape(N_HEADS * t_new, LANES), maddn_ref[...])
        flash(sn, vnew_ref[...])
        out = acc_ref[...] / l_ref[...]
        for h in range(N_HEADS):
            o_ref[h] = out[h * t_new:(h + 1) * t_new, h * HEAD_DIM:(h + 1) * HEAD_DIM]


def _sample_attention(page_table, qbd, madd, knew, vnew, bias_s, cache_kT, cache_vT, pg):
    db, n_pages = page_table.shape
    ps = cache_kT.shape[-1]
    t_new = madd.shape[1]
    assert n_pages % pg == 0

    def page_spec(u):
        return pl.BlockSpec((None, N_HEADS, HEAD_DIM, ps),
                            lambda b, g, pt: (pt[b, g * pg + u], 0, 0, 0))

    per_b = lambda b, g, pt: (b, 0, 0)
    grid_spec = pltpu.PrefetchScalarGridSpec(
        num_scalar_prefetch=1,
        grid=(db, n_pages // pg),
        in_specs=[pl.BlockSpec((None,) + qbd.shape[1:], per_b),
                  pl.BlockSpec((None, t_new, pg * ps), lambda b, g, pt: (b, 0, g)),
                  pl.BlockSpec((None, t_new, LANES), lambda b, g, pt: (b, 0, n_pages * ps // LANES)),
                  pl.BlockSpec((None,) + knew.shape[1:], per_b),
                  pl.BlockSpec((None,) + vnew.shape[1:], per_b),
                  pl.BlockSpec(bias_s.shape, lambda b, g, pt: (0, 0, 0, 0))]
                 + [page_spec(u) for u in range(pg)] * 2,
        out_specs=pl.BlockSpec((None, N_HEADS, t_new, HEAD_DIM), lambda b, g, pt: (b, 0, 0, 0)),
        scratch_shapes=[pltpu.VMEM((N_HEADS * t_new, 1), F32),
                        pltpu.VMEM((N_HEADS * t_new, 1), F32),
                        pltpu.VMEM((N_HEADS * t_new, N_HEADS * HEAD_DIM), F32)])
    return pl.pallas_call(
        functools.partial(_sattn_body, pg=pg),
        grid_spec=grid_spec,
        out_shape=jax.ShapeDtypeStruct((db, N_HEADS, t_new, HEAD_DIM), F32),
        compiler_params=pltpu.CompilerParams(dimension_semantics=("arbitrary", "arbitrary"),
                                             vmem_limit_bytes=VMEM_LIMIT),
        name="sample_attention",
    )(page_table, qbd, madd, madd, knew, vnew, bias_s, *([cache_kT] * pg), *([cache_vT] * pg))


def _outmlp_body(x_ref, a_ref, bg_ref, u_ref, prev_ref, cw_ref, wo_ref, gm_ref, wu_ref, wd_ref,
                 gf_ref, y_ref, *, seq_len, ff_chunk):
    tm = x_ref.shape[0]
    u = u_ref[...]
    w0 = cw_ref[0:1, :]
    w1 = cw_ref[1:2, :]
    w2 = cw_ref[2:3, :]
    if seq_len >= tm:
        first = (pl.program_id(0) % (seq_len // tm)) == 0
        halo = prev_ref[...] * jnp.where(first, 0.0, 1.0)
        row = lax.broadcasted_iota(I32, u.shape, 0)
        um1 = jnp.where(row == 0, halo[7:8, :], pltpu.roll(u, 1, 0))
        um2 = jnp.where(row == 0, halo[6:7, :],
                        jnp.where(row == 1, halo[7:8, :], pltpu.roll(u, 2, 0)))
    else:
        nseq = tm // seq_len
        u3 = u.reshape(nseq, seq_len, u.shape[-1])
        up = jnp.concatenate([prev_ref[...], u3], axis=1)
        um1 = up[:, 1:1 + seq_len].reshape(u.shape)
        um2 = up[:, 0:seq_len].reshape(u.shape)
    y = um2 * w0
    y = y + um1 * w1
    y = y + u * w2
    b = (bg_ref[...] * y).astype(BF16)
    ab = jnp.concatenate([a_ref[...], b], axis=-1)
    x1 = x_ref[...] + jnp.dot(ab, wo_ref[...], preferred_element_type=F32)
    ms = jnp.mean(x1 * x1, axis=-1, keepdims=True)
    hn = ((x1 * lax.rsqrt(ms + EPS)) * gm_ref[...]).astype(BF16)
    acc = jnp.zeros(x1.shape, F32)
    d_ff = wu_ref.shape[1]
    for c in range(d_ff // ff_chunk):
        sl = slice(c * ff_chunk, (c + 1) * ff_chunk)
        up_c = jnp.dot(hn, wu_ref[:, sl], preferred_element_type=F32)
        r = jnp.maximum(up_c, 0.0)
        acc = acc + jnp.dot((r * r).astype(BF16), wd_ref[sl, :], preferred_element_type=F32)
    x2 = x1 + acc
    ms2 = jnp.mean(x2 * x2, axis=-1, keepdims=True)
    y_ref[...] = (x2 * lax.rsqrt(ms2 + EPS)) * gf_ref[...]


def _out_mlp(x2d, a, bg, u, prev, conv_w, wo, g_mlp, wu, wd, g_final, tm, seq_len):
    n, d = x2d.shape
    dc = bg.shape[1]
    row = lambda i: (i, 0)
    c2 = lambda i: (0, 0)
    if seq_len >= tm:
        prev_spec = pl.BlockSpec((SUBLANES, dc),
                                 lambda i: (jnp.maximum(i * (tm // SUBLANES) - 1, 0), 0))
        prev_arg = u
    else:
        nseq = tm // seq_len
        prev_spec = pl.BlockSpec((nseq,) + prev.shape[1:], lambda i: (i, 0, 0))
        prev_arg = prev
    single = dict(pipeline_mode=pl.Buffered(1))
    return pl.pallas_call(
        functools.partial(_outmlp_body, seq_len=seq_len, ff_chunk=1024),
        grid=(n // tm,),
        in_specs=[pl.BlockSpec((tm, d), row),
                  pl.BlockSpec((tm, a.shape[1]), row),
                  pl.BlockSpec((tm, dc), row),
                  pl.BlockSpec((tm, dc), row),
                  prev_spec,
                  pl.BlockSpec(conv_w.shape, c2),
                  pl.BlockSpec(wo.shape, c2, **single),
                  pl.BlockSpec((1, d), c2),
                  pl.BlockSpec(wu.shape, c2, **single),
                  pl.BlockSpec(wd.shape, c2, **single),
                  pl.BlockSpec((1, d), c2)],
        out_specs=pl.BlockSpec((tm, d), row),
        out_shape=jax.ShapeDtypeStruct((n, d), F32),
        compiler_params=pltpu.CompilerParams(dimension_semantics=("arbitrary",),
                                             vmem_limit_bytes=VMEM_LIMIT),
        name="out_mlp",
    )(x2d, a, bg, u, prev_arg, conv_w, wo, g_mlp, wu, wd, g_final)


def _pick_tile(n, pref):
    t = min(pref, n)
    while n % t:
        t //= 2
    return t


def kernel(x_prompt, x_sample, cache_k, cache_v, cache_kidx, state_conv, page_table, rel_bias,
           g_mix, w_in, conv_w, w_out, g_mlp, w_up, w_down, g_final):
    depth = w_in.shape[0]
    assert depth == 1, "single-layer step"
    batch, seq, d_model = x_prompt.shape
    db, t_new, _ = x_sample.shape
    n_pages = page_table.shape[1]
    ps = cache_k.shape[2]
    past = n_pages * ps
    assert ps == LANES and t_new == SUBLANES

    tq = _pick_tile(seq, 256)
    n_sel_p = min(TOPK_MAX, seq // 4)
    n_sel_s = min(TOPK_MAX, (past + t_new) // 4)
    assert _far_bucket_is_constant(tq + 1, seq) and _far_bucket_is_constant(ps + 1, past + t_new)

    w = w_in[0]
    cq, ck_, cv, cqi, cki, cwi, cbg, ccg, ch = np.cumsum(
        [0, D_ATTN, D_ATTN, D_ATTN, N_IDX_HEADS * IDX_DIM, IDX_DIM, N_IDX_HEADS, D_ATTN, D_ATTN])
    end = ch + D_ATTN
    wt = jnp.pad(w[:, cq:cbg].T, ((0, T_END - cbg), (0, 0))).astype(BF16)
    wn = jnp.concatenate(
        [w[:, ck_:cv], w[:, cbg:end], jnp.pad(w[:, cki:cwi], ((0, 0), (0, LANES - IDX_DIM)))],
        axis=1).astype(BF16)
    wo = w_out[0].astype(BF16)
    wu = w_up[0].astype(BF16)
    wd = w_down[0].astype(BF16)
    gmix = g_mix[0][None]
    gmlp = g_mlp[0][None]
    gfin = g_final[None]
    cw = conv_w[0]

    n_p = batch * seq
    (qT, kT, vT, vTb, qiT, kiT, wiT, kn, kin, bg, u) = _in_proj(
        x_prompt, gmix, wn, wt, _pick_tile(seq, 512))
    bias_p = _bias_tables(rel_bias, tq, tq, (0, tq), -1, 1)
    a_p = _prompt_attention(qT, qiT, wiT, kn, kin, vTb, bias_p, tq, n_sel_p)
    y_prompt = _out_mlp(x_prompt.reshape(n_p, d_model), a_p.reshape(n_p, D_ATTN),
                        bg.reshape(n_p, D_ATTN), u.reshape(n_p, D_ATTN), None, cw, wo, gmlp, wu, wd,
                        gfin, _pick_tile(seq, 512), seq).reshape(batch, seq, d_model)

    def heads_out(t):
        b_, _, s_ = t.shape
        return t.reshape(b_, N_HEADS, HEAD_DIM, s_).transpose(0, 3, 1, 2)[None]

    k_prompt = heads_out(kT)
    v_prompt = heads_out(vT)
    kidx_prompt = kiT.transpose(0, 2, 1)[None]
    conv_prompt = u[:, seq - (CONV_WIDTH - 1):][None]

    ns = db * t_new
    (qTs, kTs, vTs, _, qiTs, kiTs, wiTs, _, _, bgs, us) = [
        t[0] for t in _in_proj(x_sample.reshape(1, ns, d_model), gmix, wn, wt, ns)]
    qi_s = qiTs.reshape(N_IDX_HEADS, IDX_DIM, db, t_new).transpose(2, 0, 3, 1).reshape(
        db, N_IDX_HEADS * t_new, IDX_DIM)
    w_s = wiTs.reshape(N_IDX_HEADS, db, t_new).transpose(1, 0, 2).reshape(db, N_IDX_HEADS * t_new, 1)
    kinew = jnp.pad(kiTs.reshape(IDX_DIM, db, t_new).transpose(1, 0, 2),
                    ((0, 0), (0, 0), (0, LANES - t_new))).astype(BF16)
    q_s = qTs.reshape(N_HEADS, HEAD_DIM, db, t_new).transpose(2, 0, 3, 1)
    eye = jnp.eye(N_HEADS, dtype=q_s.dtype)
    qbd = (q_s[:, :, :, None, :] * eye[None, :, None, :, None]).reshape(
        db, N_HEADS * t_new, N_HEADS * HEAD_DIM)
    pad_new = lambda t: jnp.pad(t.reshape(D_ATTN, db, t_new).transpose(1, 0, 2),
                                ((0, 0), (0, 0), (0, LANES - t_new))).astype(BF16)
    knew = pad_new(kTs)
    vnew = pad_new(vTs)
    kidxT = cache_kidx[0].transpose(0, 2, 1)
    cache_kT = cache_k[0].transpose(0, 2, 3, 1)
    cache_vT = cache_v[0].transpose(0, 2, 3, 1)

    madd = _sample_select(page_table, qi_s, w_s, kinew, kidxT, _pick_tile(n_pages, 128), n_sel_s,
                          _pick_tile(db, 4))
    bias_s = _bias_tables(rel_bias, t_new, LANES, (ps, 0), 1, -1)
    o_s = _sample_attention(page_table, qbd, madd, knew, vnew, bias_s, cache_kT, cache_vT,
                            _pick_tile(n_pages, 32))
    a_s = o_s.transpose(0, 2, 1, 3).reshape(ns, D_ATTN).astype(BF16)
    y_sample = _out_mlp(x_sample.reshape(ns, d_model), a_s, bgs, us, state_conv[0], cw, wo, gmlp,
                        wu, wd, gfin, ns, t_new).reshape(db, t_new, d_model)

    def heads_out_s(t):
        return t.reshape(N_HEADS, HEAD_DIM, db, t_new).transpose(2, 3, 0, 1)[None]

    k_sample = heads_out_s(kTs)
    v_sample = heads_out_s(vTs)
    kidx_sample = kiTs.reshape(IDX_DIM, db, t_new).transpose(1, 2, 0)[None]
    conv_sample = us.reshape(db, t_new, D_ATTN)[:, t_new - (CONV_WIDTH - 1):][None]

    return (y_prompt, y_sample, k_prompt, v_prompt, kidx_prompt, conv_prompt,
            k_sample, v_sample, kidx_sample, conv_sample)
```

```python
import functools
import math

import jax
import jax.numpy as jnp
import numpy as np
from jax import lax
from jax.experimental import pallas as pl
from jax.experimental.pallas import tpu as pltpu

F32 = jnp.float32
BF16 = jnp.bfloat16
I32 = jnp.int32

HEAD_DIM = 64
N_HEADS = 8
N_IDX_HEADS = 8
IDX_DIM = 32
D_ATTN = N_HEADS * HEAD_DIM
TOPK_MAX = 256
CONV_WIDTH = 3
N_BUCKETS = 32
MAX_DISTANCE = 128
EPS = 1e-6
ATTN_SCALE = HEAD_DIM ** -0.5
INDEX_SCALE = (IDX_DIM ** -0.5) * (N_IDX_HEADS ** -0.5)
LOG2E = math.log2(math.e)

LANES = 128
SUBLANES = 8
NEG = -1e30
VMEM_LIMIT = 56 * 1024 * 1024


def _float_key(v):
    b = int(np.array(v, np.float32).view(np.int32))
    return b if b >= 0 else b ^ 0x7FFFFFFF


KEY_LO = _float_key(-np.finfo(np.float32).max)
KEY_HI = _float_key(np.inf)
N_BISECT = 32
assert KEY_HI - KEY_LO < 1 << N_BISECT
COUNT_ROWS = 64
N_SBUF = N_HEADS
LOOKAHEAD = 5
assert N_IDX_HEADS == N_HEADS and 0 < LOOKAHEAD < N_SBUF


def _key_to_float(k):
    bits = k ^ ((k >> 31) & 0x7FFFFFFF)
    return lax.bitcast_convert_type(bits, F32)


def _mid(lo, hi):
    return (lo >> 1) + (hi >> 1) + (lo & hi & 1)


def _tree_sum(parts):
    parts = list(parts)
    while len(parts) > 1:
        nxt = [parts[k] + parts[k + 1] for k in range(0, len(parts) - 1, 2)]
        if len(parts) % 2:
            nxt.append(parts[-1])
        parts = nxt
    return parts[0]


def _bias_body(rb_ref, o_ref, *, offs, sa, sb):
    w = pl.program_id(0)
    h = pl.program_id(1)
    shape = o_ref.shape
    a = lax.broadcasted_iota(I32, shape, 0)
    b = lax.broadcasted_iota(I32, shape, 1)
    off = jnp.where(w == 0, offs[0], offs[1])
    dist = off + sa * a + sb * b
    n = jnp.maximum(dist, 0)
    max_exact = N_BUCKETS // 2
    nf = jnp.maximum(n, 1).astype(F32)
    large = max_exact + jnp.floor(jnp.log(nf / max_exact) / math.log(MAX_DISTANCE / max_exact)
                                  * (N_BUCKETS - max_exact)).astype(I32)
    large = jnp.minimum(large, N_BUCKETS - 1)
    bucket = jnp.where(n < max_exact, n, large)
    val = jnp.zeros(shape, F32)
    for k in range(N_BUCKETS):
        val = jnp.where(bucket == k, rb_ref[k, h], val)
    o_ref[...] = (val - rb_ref[N_BUCKETS - 1, h]) * LOG2E


def _bias_tables(rel_bias, rows, cols, offs, sa, sb):
    return pl.pallas_call(
        functools.partial(_bias_body, offs=offs, sa=sa, sb=sb),
        grid=(2, N_HEADS),
        in_specs=[pl.BlockSpec(memory_space=pltpu.SMEM)],
        out_specs=pl.BlockSpec((None, None, rows, cols), lambda w, h: (w, h, 0, 0)),
        out_shape=jax.ShapeDtypeStruct((2, N_HEADS, rows, cols), F32),
        name="bias_tables",
    )(rel_bias)


def _far_bucket_is_constant(min_dist, max_dist):
    d = np.arange(min_dist, max_dist + 1, dtype=np.float64)
    me = N_BUCKETS // 2
    b = me + np.floor(np.log(d / me) / math.log(MAX_DISTANCE / me) * (N_BUCKETS - me))
    return bool(np.all(np.minimum(b, N_BUCKETS - 1) == N_BUCKETS - 1)) and min_dist >= me


T_Q, T_K, T_V, T_QI, T_KI, T_WI, T_END = 0, 512, 1024, 1536, 1792, 1824, 1840
N_K, N_BG, N_CG, N_H, N_KI, N_END = 0, 512, 1024, 1536, 2048, 2176


def _inproj_body(x_ref, g_ref, wn_ref, wt_ref,
                 qT_ref, kT_ref, vT_ref, vTb_ref, qiT_ref, kiT_ref, wiT_ref,
                 kn_ref, kin_ref, bg_ref, u_ref):
    x = x_ref[...]
    ms = jnp.mean(x * x, axis=-1, keepdims=True)
    xn = ((x * lax.rsqrt(ms + EPS)) * g_ref[...]).astype(BF16)

    def nat(a, b):
        return jnp.dot(xn, wn_ref[:, a:b], preferred_element_type=F32)

    zt = lax.dot_general(wt_ref[...], xn, (((1,), (1,)), ((), ())), preferred_element_type=F32)

    def tra(a, b):
        return zt[a:b, :]

    kn_ref[...] = nat(N_K, N_BG).astype(BF16)
    bg_ref[...] = nat(N_BG, N_CG)
    u_ref[...] = nat(N_CG, N_H) * nat(N_H, N_KI)
    kin_ref[...] = nat(N_KI, N_END).astype(BF16)

    qT_ref[...] = (tra(T_Q, T_K) * (ATTN_SCALE * LOG2E)).astype(BF16)
    kT_ref[...] = tra(T_K, T_V)
    vt = tra(T_V, T_QI)
    vT_ref[...] = vt
    vTb_ref[...] = vt.astype(BF16)
    qiT_ref[...] = tra(T_QI, T_KI).astype(BF16)
    kiT_ref[...] = tra(T_KI, T_WI)
    wiT_ref[...] = tra(T_WI, T_END)[0:N_IDX_HEADS, :] * INDEX_SCALE


def _in_proj(x3, g, wn, wt, tm):
    nb, seq, d = x3.shape
    assert seq % tm == 0
    const = lambda b, i: (0, 0)
    row = lambda b, i: (b, i, 0)
    col = lambda b, i: (b, 0, i)
    di = N_IDX_HEADS * IDX_DIM
    outs = [
        ((nb, D_ATTN, seq), BF16, (None, D_ATTN, tm), col),
        ((nb, D_ATTN, seq), F32, (None, D_ATTN, tm), col),
        ((nb, D_ATTN, seq), F32, (None, D_ATTN, tm), col),
        ((nb, D_ATTN, seq), BF16, (None, D_ATTN, tm), col),
        ((nb, di, seq), BF16, (None, di, tm), col),
        ((nb, IDX_DIM, seq), F32, (None, IDX_DIM, tm), col),
        ((nb, N_IDX_HEADS, seq), F32, (None, N_IDX_HEADS, tm), col),
        ((nb, seq, D_ATTN), BF16, (None, tm, D_ATTN), row),
        ((nb, seq, LANES), BF16, (None, tm, LANES), row),
        ((nb, seq, D_ATTN), F32, (None, tm, D_ATTN), row),
        ((nb, seq, D_ATTN), F32, (None, tm, D_ATTN), row),
    ]
    return pl.pallas_call(
        _inproj_body,
        grid=(nb, seq // tm),
        in_specs=[pl.BlockSpec((None, tm, d), row),
                  pl.BlockSpec((1, d), const),
                  pl.BlockSpec(wn.shape, const),
                  pl.BlockSpec(wt.shape, const)],
        out_specs=[pl.BlockSpec(bs, im) for (_, _, bs, im) in outs],
        out_shape=[jax.ShapeDtypeStruct(s, dt) for (s, dt, _, _) in outs],
        compiler_params=pltpu.CompilerParams(dimension_semantics=("arbitrary", "arbitrary"),
                                             vmem_limit_bytes=VMEM_LIMIT),
        name="in_proj",
    )(x3, g, wn, wt)


def _pattn_body(qT_ref, qiT_ref, wiT_ref, kn_ref, kin_ref, vT_ref, bias_ref, a_ref,
                it_ref, qz_ref, m_ref, l_ref, acc_ref, p_ref, s_ref,
                *, tq, n_sel):
    i = pl.program_id(1)
    nch = i + 1
    ck = tq
    s_tot = kn_ref.shape[0]
    q_idx = i * tq + lax.broadcasted_iota(I32, (ck, tq), 1)
    row_iota = lax.broadcasted_iota(I32, (ck, tq), 0)

    def idx_dot(j, h):
        r0 = pl.multiple_of(j * ck, ck)
        ki = kin_ref[pl.ds(r0, ck), :][:, 0:IDX_DIM]
        return jnp.dot(ki, qiT_ref[h * IDX_DIM:(h + 1) * IDX_DIM, :],
                       preferred_element_type=F32)

    for h in range(LOOKAHEAD):
        s_ref[h] = idx_dot(0, h)

    def idx_chunk(j, carry):
        r0 = pl.multiple_of(j * ck, ck)
        jn = jnp.minimum(j + 1, nch - 1)
        terms = []
        for h in range(N_IDX_HEADS):
            ahead = h + LOOKAHEAD
            if ahead < N_IDX_HEADS:
                s_ref[ahead] = idx_dot(j, ahead)
            else:
                s_ref[ahead - N_IDX_HEADS] = idx_dot(jn, ahead - N_IDX_HEADS)
            terms.append(wiT_ref[h:h + 1, :] * jnp.maximum(s_ref[h], 0.0))
        acc = _tree_sum(terms)
        acc = jnp.where(r0 + row_iota <= q_idx, acc, -jnp.inf)
        it_ref[pl.ds(r0, ck), :] = acc
        return carry

    lax.fori_loop(0, nch, idx_chunk, 0)

    ck2 = 2 * ck
    row_iota2 = lax.broadcasted_iota(I32, (COUNT_ROWS, tq), 0)

    def count(pred):
        def blocks(r0, n_rows, c8):
            for sb in range(n_rows // COUNT_ROWS):
                base = r0 + sb * COUNT_ROWS
                x = it_ref[pl.ds(base, COUNT_ROWS), :]
                hit = jnp.where(pred(x, base), 1.0, 0.0)
                c8 = c8 + _tree_sum(hit[r * SUBLANES:(r + 1) * SUBLANES]
                                    for r in range(COUNT_ROWS // SUBLANES))
            return c8

        def pair(t, c8):
            return blocks(pl.multiple_of(t * ck2, ck2), ck2, c8)

        def odd(_, c8):
            return blocks(pl.multiple_of((nch - 1) * ck, ck), ck, c8)

        c8 = lax.fori_loop(0, nch >> 1, pair, jnp.zeros((SUBLANES, tq), F32))
        c8 = lax.fori_loop(0, nch & 1, odd, c8)
        return c8.sum(axis=0, keepdims=True)

    def bis(_, st):
        lo, hi, c_lo, c_hi = st
        mid = _mid(lo, hi)
        midf = _key_to_float(mid)
        c = count(lambda x, r0: x >= midf)
        ok = c >= n_sel
        return (jnp.where(ok, mid, lo), jnp.where(ok, hi, mid),
                jnp.where(ok, c, c_lo), jnp.where(ok, c_hi, c))

    zero = jnp.zeros((1, tq), F32)
    lo, _, c_ge, c_gt = lax.fori_loop(
        0, N_BISECT, bis,
        (jnp.full((1, tq), KEY_LO, I32), jnp.full((1, tq), KEY_HI, I32), zero, zero))
    tau = _key_to_float(lo)
    need = n_sel - c_gt

    p_ref[...] = jnp.full((1, tq), s_tot, I32)

    @pl.when(jnp.max(c_ge) > n_sel)
    def _():
        def tie(_, lohi):
            plo, phi = lohi
            pm = (plo + phi) >> 1
            ok = count(lambda x, r0: (x == tau) & (r0 + row_iota2 <= pm)) >= need
            return jnp.where(ok, plo, pm), jnp.where(ok, pm, phi)
        n_it = int(math.ceil(math.log2(s_tot))) + 1
        _, phi = lax.fori_loop(0, n_it, tie,
                               (jnp.full((1, tq), -1, I32), jnp.full((1, tq), s_tot - 1, I32)))
        p_ref[...] = phi

    pcut = p_ref[...]

    def mask_chunk(j, carry):
        r0 = pl.multiple_of(j * ck, ck)
        x = it_ref[pl.ds(r0, ck), :]
        sel = (x > tau) | ((x == tau) & (r0 + row_iota <= pcut))
        it_ref[pl.ds(r0, ck), :] = jnp.where(sel, 0.0, NEG)
        return carry

    lax.fori_loop(0, nch, mask_chunk, 0)

    half = lax.broadcasted_iota(I32, (2 * HEAD_DIM, tq), 0) // HEAD_DIM
    for h in range(N_HEADS):
        pair = qT_ref[(h // 2) * 2 * HEAD_DIM:(h // 2 + 1) * 2 * HEAD_DIM, :]
        qz_ref[h] = jnp.where(half == (h % 2), pair, jnp.zeros_like(pair))
    m_ref[...] = jnp.full(m_ref.shape, NEG, F32)
    l_ref[...] = jnp.zeros(l_ref.shape, F32)
    acc_ref[...] = jnp.zeros(acc_ref.shape, F32)

    def qk(j, h):
        r0 = pl.multiple_of(j * ck, ck)
        kp = kn_ref[pl.ds(r0, ck), (h // 2) * 2 * HEAD_DIM:(h // 2 + 1) * 2 * HEAD_DIM]
        return jnp.dot(kp, qz_ref[h], preferred_element_type=F32) + it_ref[pl.ds(r0, ck), :]

    for h in range(LOOKAHEAD):
        s_ref[h] = qk(0, h)

    def attend(j, bias_w, has_next):
        r0 = pl.multiple_of(j * ck, ck)
        ones = jnp.ones((2 * SUBLANES, ck), BF16)
        for h in range(N_HEADS):
            ahead = h + LOOKAHEAD
            if ahead < N_HEADS:
                s_ref[ahead] = qk(j, ahead)
            elif has_next:
                s_ref[ahead - N_HEADS] = qk(j + 1, ahead - N_HEADS)
            s = s_ref[h]
            if bias_w is not None:
                s = s + bias_ref[bias_w, h]
            m_old = m_ref[h:h + 1, :]
            m_new = jnp.maximum(m_old, jnp.max(s, axis=0, keepdims=True))
            p = jnp.exp2(s - m_new)
            alpha = jnp.exp2(m_old - m_new)
            vt = vT_ref[h * HEAD_DIM:(h + 1) * HEAD_DIM, pl.ds(r0, ck)]
            pv = jnp.dot(jnp.concatenate([vt, ones], axis=0), p.astype(BF16),
                         preferred_element_type=F32)
            l_ref[h:h + 1, :] = alpha * l_ref[h:h + 1, :] + pv[HEAD_DIM:HEAD_DIM + 1, :]
            acc_ref[h * HEAD_DIM:(h + 1) * HEAD_DIM, :] = (
                alpha * acc_ref[h * HEAD_DIM:(h + 1) * HEAD_DIM, :] + pv[0:HEAD_DIM, :])
            m_ref[h:h + 1, :] = m_new

    def far_chunk(j, carry):
        attend(j, None, True)
        return carry

    lax.fori_loop(0, jnp.maximum(i - 1, 0), far_chunk, 0)

    @pl.when(i >= 1)
    def _():
        attend(i - 1, 1, True)

    attend(i, 0, False)

    for h in range(N_HEADS):
        sl = slice(h * HEAD_DIM, (h + 1) * HEAD_DIM)
        acc_ref[sl, :] = acc_ref[sl, :] / l_ref[h:h + 1, :]
    a_ref[...] = acc_ref[...].T.astype(BF16)


def _prompt_attention(qT, qiT, wiT, kn, kin, vTb, bias, tq, n_sel):
    batch, seq, _ = kn.shape
    nq = seq // tq
    blk = lambda b, i: (b, 0, i)
    per_b = lambda b, i: (b, 0, 0)
    return pl.pallas_call(
        functools.partial(_pattn_body, tq=tq, n_sel=n_sel),
        grid=(batch, nq),
        in_specs=[pl.BlockSpec((None, D_ATTN, tq), blk),
                  pl.BlockSpec((None, N_IDX_HEADS * IDX_DIM, tq), blk),
                  pl.BlockSpec((None, N_IDX_HEADS, tq), blk),
                  pl.BlockSpec((None, seq, D_ATTN), per_b),
                  pl.BlockSpec((None, seq, LANES), per_b),
                  pl.BlockSpec((None, D_ATTN, seq), per_b),
                  pl.BlockSpec(bias.shape, lambda b, i: (0, 0, 0, 0))],
        out_specs=pl.BlockSpec((None, tq, D_ATTN), lambda b, i: (b, i, 0)),
        out_shape=jax.ShapeDtypeStruct((batch, seq, D_ATTN), BF16),
        scratch_shapes=[pltpu.VMEM((seq, tq), F32),
                        pltpu.VMEM((N_HEADS, 2 * HEAD_DIM, tq), BF16),
                        pltpu.VMEM((N_HEADS, tq), F32),
                        pltpu.VMEM((N_HEADS, tq), F32),
                        pltpu.VMEM((D_ATTN, tq), F32),
                        pltpu.VMEM((1, tq), I32),
                        pltpu.VMEM((N_SBUF, tq, tq), F32)],
        compiler_params=pltpu.CompilerParams(dimension_semantics=("arbitrary", "arbitrary"),
                                             vmem_limit_bytes=VMEM_LIMIT),
        name="prompt_attention",
    )(qT, qiT, wiT, kn, kin, vTb, bias)


def _sidx_body(pt_ref, qi_ref, w_ref, kinew_ref, *rest, pg, n_pages, t_new, n_sel, group):
    pages = rest[:pg]
    madd_ref = rest[pg]
    it_ref, p_ref = rest[pg + 1:]
    b = pl.program_id(0)
    g = pl.program_id(1)
    ps = pages[0].shape[-1]
    past = n_pages * ps
    tot = past + LANES
    n_slab = tot // LANES
    rows = group * t_new
    rb = pl.multiple_of((b % group) * t_new, t_new)
    qi = qi_ref[...]
    w = w_ref[...]

    def scores(kt):
        s = jnp.dot(qi, kt.astype(BF16), preferred_element_type=F32)
        r = jnp.maximum(s, 0.0) * w
        return _tree_sum(r[h * t_new:(h + 1) * t_new] for h in range(N_IDX_HEADS))

    kt = jnp.concatenate([p[...] for p in pages], axis=-1)
    c0 = pl.multiple_of(g * (pg * ps), pg * ps)
    it_ref[pl.ds(rb, t_new), pl.ds(c0, pg * ps)] = scores(kt)
    last_g = g == pl.num_programs(1) - 1

    @pl.when(last_g)
    def _():
        sn = scores(kinew_ref[...].astype(F32))
        tq_i = lax.broadcasted_iota(I32, (t_new, LANES), 0)
        tk_i = lax.broadcasted_iota(I32, (t_new, LANES), 1)
        it_ref[pl.ds(rb, t_new), past:tot] = jnp.where(tk_i <= tq_i, sn, -jnp.inf)

    @pl.when(last_g & (b % group == group - 1))
    def _():
        lane_i = lax.broadcasted_iota(I32, (rows, LANES), 1)

        def count(pred):
            accs = [jnp.zeros((rows, LANES), F32) for _ in range(2)]
            for sl in range(n_slab):
                x = it_ref[:, sl * LANES:(sl + 1) * LANES]
                accs[sl % 2] = accs[sl % 2] + jnp.where(pred(x, sl * LANES), 1.0, 0.0)
            return jnp.sum(accs[0] + accs[1], axis=1, keepdims=True)

        def bis(_, st):
            lo, hi, c_lo, c_hi = st
            mid = _mid(lo, hi)
            midf = _key_to_float(mid)
            c = count(lambda x, c0_: x >= midf)
            ok = c >= n_sel
            return (jnp.where(ok, mid, lo), jnp.where(ok, hi, mid),
                    jnp.where(ok, c, c_lo), jnp.where(ok, c_hi, c))

        zero = jnp.zeros((rows, 1), F32)
        lo, _, c_ge, c_gt = lax.fori_loop(
            0, N_BISECT, bis,
            (jnp.full((rows, 1), KEY_LO, I32), jnp.full((rows, 1), KEY_HI, I32), zero, zero))
        tau = _key_to_float(lo)
        need = n_sel - c_gt
        p_ref[...] = jnp.full((rows, 1), tot, I32)

        @pl.when(jnp.max(c_ge) > n_sel)
        def _():
            def tie(_, lohi):
                plo, phi = lohi
                pm = (plo + phi) >> 1
                ok = count(lambda x, c0_: (x == tau) & (c0_ + lane_i <= pm)) >= need
                return jnp.where(ok, plo, pm), jnp.where(ok, pm, phi)
            n_it = int(math.ceil(math.log2(tot))) + 1
            _, phi = lax.fori_loop(0, n_it, tie, (jnp.full((rows, 1), -1, I32),
                                                  jnp.full((rows, 1), tot - 1, I32)))
            p_ref[...] = phi

        pcut = p_ref[...]
        for sl in range(n_slab):
            x = it_ref[:, sl * LANES:(sl + 1) * LANES]
            sel = (x > tau) | ((x == tau) & (sl * LANES + lane_i <= pcut))
            madd_ref[:, :, sl * LANES:(sl + 1) * LANES] = jnp.where(sel, 0.0, NEG).reshape(
                group, t_new, LANES)


def _sample_select(page_table, qi_s, w_s, kinew, kidxT, pg, n_sel, group):
    db, n_pages = page_table.shape
    ps = kidxT.shape[-1]
    t_new = qi_s.shape[1] // N_IDX_HEADS
    tot = n_pages * ps + LANES
    assert n_pages % pg == 0 and ps == LANES and db % group == 0

    def page_spec(u):
        return pl.BlockSpec((None, IDX_DIM, ps), lambda b, g, pt: (pt[b, g * pg + u], 0, 0))

    grid_spec = pltpu.PrefetchScalarGridSpec(
        num_scalar_prefetch=1,
        grid=(db, n_pages // pg),
        in_specs=[pl.BlockSpec((None,) + qi_s.shape[1:], lambda b, g, pt: (b, 0, 0)),
                  pl.BlockSpec((None,) + w_s.shape[1:], lambda b, g, pt: (b, 0, 0)),
                  pl.BlockSpec((None,) + kinew.shape[1:], lambda b, g, pt: (b, 0, 0))]
                 + [page_spec(u) for u in range(pg)],
        out_specs=pl.BlockSpec((group, t_new, tot), lambda b, g, pt: (b // group, 0, 0)),
        scratch_shapes=[pltpu.VMEM((group * t_new, tot), F32),
                        pltpu.VMEM((group * t_new, 1), I32)])
    return pl.pallas_call(
        functools.partial(_sidx_body, pg=pg, n_pages=n_pages, t_new=t_new, n_sel=n_sel,
                          group=group),
        grid_spec=grid_spec,
        out_shape=jax.ShapeDtypeStruct((db, t_new, tot), F32),
        compiler_params=pltpu.CompilerParams(dimension_semantics=("arbitrary", "arbitrary"),
                                             vmem_limit_bytes=VMEM_LIMIT),
        name="sample_select",
    )(page_table, qi_s, w_s, kinew, *([kidxT] * pg))


def _sattn_body(pt_ref, qbd_ref, madd_ref, maddn_ref, knew_ref, vnew_ref, bias_ref, *rest, pg):
    kpages = rest[:pg]
    vpages = rest[pg:2 * pg]
    o_ref = rest[2 * pg]
    m_ref, l_ref, acc_ref = rest[2 * pg + 1:]
    g = pl.program_id(1)
    last = pl.num_programs(1) - 1
    qbd = qbd_ref[...]
    ps = kpages[0].shape[-1]
    t_new = madd_ref.shape[0]
    hd = N_HEADS * HEAD_DIM

    @pl.when(g == 0)
    def _():
        m_ref[...] = jnp.full(m_ref.shape, NEG, F32)
        l_ref[...] = jnp.zeros(l_ref.shape, F32)
        acc_ref[...] = jnp.zeros(acc_ref.shape, F32)

    def flash(s, vt):
        m_old = m_ref[...]
        m_new = jnp.maximum(m_old, jnp.max(s, axis=-1, keepdims=True))
        p = jnp.exp2(s - m_new)
        alpha = jnp.exp2(m_old - m_new)
        l_ref[...] = alpha * l_ref[...] + jnp.sum(p, axis=-1, keepdims=True)
        pv = lax.dot_general(p.astype(BF16), vt, (((1,), (1,)), ((), ())),
                             preferred_element_type=F32)
        acc_ref[...] = alpha * acc_ref[...] + pv
        m_ref[...] = m_new

    def add_rows(s, add):
        n = s.shape[-1]
        return (s.reshape(N_HEADS, t_new, n) + add[None]).reshape(N_HEADS * t_new, n)

    kt = jnp.concatenate([kp[...].reshape(hd, ps) for kp in kpages], axis=-1).astype(BF16)
    vt = jnp.concatenate([vp[...].reshape(hd, ps) for vp in vpages], axis=-1).astype(BF16)
    s = jnp.dot(qbd, kt, preferred_element_type=F32)
    s = add_rows(s, madd_ref[...])
    is_last = jnp.where(g == last, 1.0, 0.0)
    tail = s[:, (pg - 1) * ps:] + is_last * bias_ref[0].reshape(N_HEADS * t_new, ps)
    s = jnp.concatenate([s[:, :(pg - 1) * ps], tail], axis=-1)
    flash(s, vt)

    @pl.when(g == last)
    def _():
        sn = jnp.dot(qbd, knew_ref[...], preferred_element_type=F32)
        sn = add_rows(sn + bias_ref[1].reshape(N_HEADS * t_new, LANES), maddn_ref[...])
        flash(sn, vnew_ref[...])
        out = acc_ref[...] / l_ref[...]
        for h in range(N_HEADS):
            o_ref[h] = out[h * t_new:(h + 1) * t_new, h * HEAD_DIM:(h + 1) * HEAD_DIM]


def _sample_attention(page_table, qbd, madd, knew, vnew, bias_s, cache_kT, cache_vT, pg):
    db, n_pages = page_table.shape
    ps = cache_kT.shape[-1]
    t_new = madd.shape[1]
    assert n_pages % pg == 0

    def page_spec(u):
        return pl.BlockSpec((None, N_HEADS, HEAD_DIM, ps),
                            lambda b, g, pt: (pt[b, g * pg + u], 0, 0, 0))

    per_b = lambda b, g, pt: (b, 0, 0)
    grid_spec = pltpu.PrefetchScalarGridSpec(
        num_scalar_prefetch=1,
        grid=(db, n_pages // pg),
        in_specs=[pl.BlockSpec((None,) + qbd.shape[1:], per_b),
                  pl.BlockSpec((None, t_new, pg * ps), lambda b, g, pt: (b, 0, g)),
                  pl.BlockSpec((None, t_new, LANES), lambda b, g, pt: (b, 0, n_pages * ps // LANES)),
                  pl.BlockSpec((None,) + knew.shape[1:], per_b),
                  pl.BlockSpec((None,) + vnew.shape[1:], per_b),
                  pl.BlockSpec(bias_s.shape, lambda b, g, pt: (0, 0, 0, 0))]
                 + [page_spec(u) for u in range(pg)] * 2,
        out_specs=pl.BlockSpec((None, N_HEADS, t_new, HEAD_DIM), lambda b, g, pt: (b, 0, 0, 0)),
        scratch_shapes=[pltpu.VMEM((N_HEADS * t_new, 1), F32),
                        pltpu.VMEM((N_HEADS * t_new, 1), F32),
                        pltpu.VMEM((N_HEADS * t_new, N_HEADS * HEAD_DIM), F32)])
    return pl.pallas_call(
        functools.partial(_sattn_body, pg=pg),
        grid_spec=grid_spec,
        out_shape=jax.ShapeDtypeStruct((db, N_HEADS, t_new, HEAD_DIM), F32),
        compiler_params=pltpu.CompilerParams(dimension_semantics=("arbitrary", "arbitrary"),
                                             vmem_limit_bytes=VMEM_LIMIT),
        name="sample_attention",
    )(page_table, qbd, madd, madd, knew, vnew, bias_s, *([cache_kT] * pg), *([cache_vT] * pg))


def _outmlp_body(x_ref, a_ref, bg_ref, u_ref, prev_ref, cw_ref, wo_ref, gm_ref, wu_ref, wd_ref,
                 gf_ref, y_ref, *, seq_len, ff_chunk):
    tm = x_ref.shape[0]
    u = u_ref[...]
    w0 = cw_ref[0:1, :]
    w1 = cw_ref[1:2, :]
    w2 = cw_ref[2:3, :]
    if seq_len >= tm:
        first = (pl.program_id(0) % (seq_len // tm)) == 0
        halo = prev_ref[...] * jnp.where(first, 0.0, 1.0)
        row = lax.broadcasted_iota(I32, u.shape, 0)
        um1 = jnp.where(row == 0, halo[7:8, :], pltpu.roll(u, 1, 0))
        um2 = jnp.where(row == 0, halo[6:7, :],
                        jnp.where(row == 1, halo[7:8, :], pltpu.roll(u, 2, 0)))
    else:
        nseq = tm // seq_len
        u3 = u.reshape(nseq, seq_len, u.shape[-1])
        up = jnp.concatenate([prev_ref[...], u3], axis=1)
        um1 = up[:, 1:1 + seq_len].reshape(u.shape)
        um2 = up[:, 0:seq_len].reshape(u.shape)
    y = um2 * w0
    y = y + um1 * w1
    y = y + u * w2
    b = (bg_ref[...] * y).astype(BF16)
    ab = jnp.concatenate([a_ref[...], b], axis=-1)
    x1 = x_ref[...] + jnp.dot(ab, wo_ref[...], preferred_element_type=F32)
    ms = jnp.mean(x1 * x1, axis=-1, keepdims=True)
    hn = ((x1 * lax.rsqrt(ms + EPS)) * gm_ref[...]).astype(BF16)
    acc = jnp.zeros(x1.shape, F32)
    d_ff = wu_ref.shape[1]
    for c in range(d_ff // ff_chunk):
        sl = slice(c * ff_chunk, (c + 1) * ff_chunk)
        up_c = jnp.dot(hn, wu_ref[:, sl], preferred_element_type=F32)
        r = jnp.maximum(up_c, 0.0)
        acc = acc + jnp.dot((r * r).astype(BF16), wd_ref[sl, :], preferred_element_type=F32)
    x2 = x1 + acc
    ms2 = jnp.mean(x2 * x2, axis=-1, keepdims=True)
    y_ref[...] = (x2 * lax.rsqrt(ms2 + EPS)) * gf_ref[...]


def _out_mlp(x2d, a, bg, u, prev, conv_w, wo, g_mlp, wu, wd, g_final, tm, seq_len):
    n, d = x2d.shape
    dc = bg.shape[1]
    row = lambda i: (i, 0)
    c2 = lambda i: (0, 0)
    if seq_len >= tm:
        prev_spec = pl.BlockSpec((SUBLANES, dc),
                                 lambda i: (jnp.maximum(i * (tm // SUBLANES) - 1, 0), 0))
        prev_arg = u
    else:
        nseq = tm // seq_len
        prev_spec = pl.BlockSpec((nseq,) + prev.shape[1:], lambda i: (i, 0, 0))
        prev_arg = prev
    single = dict(pipeline_mode=pl.Buffered(1))
    return pl.pallas_call(
        functools.partial(_outmlp_body, seq_len=seq_len, ff_chunk=1024),
        grid=(n // tm,),
        in_specs=[pl.BlockSpec((tm, d), row),
                  pl.BlockSpec((tm, a.shape[1]), row),
                  pl.BlockSpec((tm, dc), row),
                  pl.BlockSpec((tm, dc), row),
                  prev_spec,
                  pl.BlockSpec(conv_w.shape, c2),
                  pl.BlockSpec(wo.shape, c2, **single),
                  pl.BlockSpec((1, d), c2),
                  pl.BlockSpec(wu.shape, c2, **single),
                  pl.BlockSpec(wd.shape, c2, **single),
                  pl.BlockSpec((1, d), c2)],
        out_specs=pl.BlockSpec((tm, d), row),
        out_shape=jax.ShapeDtypeStruct((n, d), F32),
        compiler_params=pltpu.CompilerParams(dimension_semantics=("arbitrary",),
                                             vmem_limit_bytes=VMEM_LIMIT),
        name="out_mlp",
    )(x2d, a, bg, u, prev_arg, conv_w, wo, g_mlp, wu, wd, g_final)


def _pick_tile(n, pref):
    t = min(pref, n)
    while n % t:
        t //= 2
    return t


def kernel(x_prompt, x_sample, cache_k, cache_v, cache_kidx, state_conv, page_table, rel_bias,
           g_mix, w_in, conv_w, w_out, g_mlp, w_up, w_down, g_final):
    depth = w_in.shape[0]
    assert depth == 1, "single-layer step"
    batch, seq, d_model = x_prompt.shape
    db, t_new, _ = x_sample.shape
    n_pages = page_table.shape[1]
    ps = cache_k.shape[2]
    past = n_pages * ps
    assert ps == LANES and t_new == SUBLANES

    tq = _pick_tile(seq, 256)
    n_sel_p = min(TOPK_MAX, seq // 4)
    n_sel_s = min(TOPK_MAX, (past + t_new) // 4)
    assert _far_bucket_is_constant(tq + 1, seq) and _far_bucket_is_constant(ps + 1, past + t_new)

    w = w_in[0]
    cq, ck_, cv, cqi, cki, cwi, cbg, ccg, ch = np.cumsum(
        [0, D_ATTN, D_ATTN, D_ATTN, N_IDX_HEADS * IDX_DIM, IDX_DIM, N_IDX_HEADS, D_ATTN, D_ATTN])
    end = ch + D_ATTN
    wt = jnp.pad(w[:, cq:cbg].T, ((0, T_END - cbg), (0, 0))).astype(BF16)
    wn = jnp.concatenate(
        [w[:, ck_:cv], w[:, cbg:end], jnp.pad(w[:, cki:cwi], ((0, 0), (0, LANES - IDX_DIM)))],
        axis=1).astype(BF16)
    wo = w_out[0].astype(BF16)
    wu = w_up[0].astype(BF16)
    wd = w_down[0].astype(BF16)
    gmix = g_mix[0][None]
    gmlp = g_mlp[0][None]
    gfin = g_final[None]
    cw = conv_w[0]

    n_p = batch * seq
    (qT, kT, vT, vTb, qiT, kiT, wiT, kn, kin, bg, u) = _in_proj(
        x_prompt, gmix, wn, wt, _pick_tile(seq, 512))
    bias_p = _bias_tables(rel_bias, tq, tq, (0, tq), -1, 1)
    a_p = _prompt_attention(qT, qiT, wiT, kn, kin, vTb, bias_p, tq, n_sel_p)
    y_prompt = _out_mlp(x_prompt.reshape(n_p, d_model), a_p.reshape(n_p, D_ATTN),
                        bg.reshape(n_p, D_ATTN), u.reshape(n_p, D_ATTN), None, cw, wo, gmlp, wu, wd,
                        gfin, _pick_tile(seq, 512), seq).reshape(batch, seq, d_model)

    def heads_out(t):
        b_, _, s_ = t.shape
        return t.reshape(b_, N_HEADS, HEAD_DIM, s_).transpose(0, 3, 1, 2)[None]

    k_prompt = heads_out(kT)
    v_prompt = heads_out(vT)
    kidx_prompt = kiT.transpose(0, 2, 1)[None]
    conv_prompt = u[:, seq - (CONV_WIDTH - 1):][None]

    ns = db * t_new
    (qTs, kTs, vTs, _, qiTs, kiTs, wiTs, _, _, bgs, us) = [
        t[0] for t in _in_proj(x_sample.reshape(1, ns, d_model), gmix, wn, wt, ns)]
    qi_s = qiTs.reshape(N_IDX_HEADS, IDX_DIM, db, t_new).transpose(2, 0, 3, 1).reshape(
        db, N_IDX_HEADS * t_new, IDX_DIM)
    w_s = wiTs.reshape(N_IDX_HEADS, db, t_new).transpose(1, 0, 2).reshape(db, N_IDX_HEADS * t_new, 1)
    kinew = jnp.pad(kiTs.reshape(IDX_DIM, db, t_new).transpose(1, 0, 2),
                    ((0, 0), (0, 0), (0, LANES - t_new))).astype(BF16)
    q_s = qTs.reshape(N_HEADS, HEAD_DIM, db, t_new).transpose(2, 0, 3, 1)
    eye = jnp.eye(N_HEADS, dtype=q_s.dtype)
    qbd = (q_s[:, :, :, None, :] * eye[None, :, None, :, None]).reshape(
        db, N_HEADS * t_new, N_HEADS * HEAD_DIM)
    pad_new = lambda t: jnp.pad(t.reshape(D_ATTN, db, t_new).transpose(1, 0, 2),
                                ((0, 0), (0, 0), (0, LANES - t_new))).astype(BF16)
    knew = pad_new(kTs)
    vnew = pad_new(vTs)
    kidxT = cache_kidx[0].transpose(0, 2, 1)
    cache_kT = cache_k[0].transpose(0, 2, 3, 1)
    cache_vT = cache_v[0].transpose(0, 2, 3, 1)

    madd = _sample_select(page_table, qi_s, w_s, kinew, kidxT, _pick_tile(n_pages, 128), n_sel_s,
                          _pick_tile(db, 4))
    bias_s = _bias_tables(rel_bias, t_new, LANES, (ps, 0), 1, -1)
    o_s = _sample_attention(page_table, qbd, madd, knew, vnew, bias_s, cache_kT, cache_vT,
                            _pick_tile(n_pages, 32))
    a_s = o_s.transpose(0, 2, 1, 3).reshape(ns, D_ATTN).astype(BF16)
    y_sample = _out_mlp(x_sample.reshape(ns, d_model), a_s, bgs, us, state_conv[0], cw, wo, gmlp,
                        wu, wd, gfin, ns, t_new).reshape(db, t_new, d_model)

    def heads_out_s(t):
        return t.reshape(N_HEADS, HEAD_DIM, db, t_new).transpose(2, 3, 0, 1)[None]

    k_sample = heads_out_s(kTs)
    v_sample = heads_out_s(vTs)
    kidx_sample = kiTs.reshape(IDX_DIM, db, t_new).transpose(1, 2, 0)[None]
    conv_sample = us.reshape(db, t_new, D_ATTN)[:, t_new - (CONV_WIDTH - 1):][None]

    return (y_prompt, y_sample, k_prompt, v_prompt, kidx_prompt, conv_prompt,
            k_sample, v_sample, kidx_sample, conv_sample)
```

```python
import functools
import math

import jax
import jax.numpy as jnp
import numpy as np
from jax import lax
from jax.experimental import pallas as pl
from jax.experimental.pallas import tpu as pltpu

F32 = jnp.float32
BF16 = jnp.bfloat16
I32 = jnp.int32

HEAD_DIM = 64
N_HEADS = 8
N_IDX_HEADS = 8
IDX_DIM = 32
D_ATTN = N_HEADS * HEAD_DIM
TOPK_MAX = 256
CONV_WIDTH = 3
N_BUCKETS = 32
MAX_DISTANCE = 128
EPS = 1e-6
ATTN_SCALE = HEAD_DIM ** -0.5
INDEX_SCALE = (IDX_DIM ** -0.5) * (N_IDX_HEADS ** -0.5)
LOG2E = math.log2(math.e)

LANES = 128
SUBLANES = 8
NEG = -1e30
VMEM_LIMIT = 56 * 1024 * 1024


def _float_key(v):
    b = int(np.array(v, np.float32).view(np.int32))
    return b if b >= 0 else b ^ 0x7FFFFFFF


KEY_LO = _float_key(-np.finfo(np.float32).max)
KEY_HI = _float_key(np.inf)
N_BISECT = 32
assert KEY_HI - KEY_LO < 1 << N_BISECT
COUNT_ROWS = 64
N_SBUF = N_HEADS
LOOKAHEAD = 5
assert N_IDX_HEADS == N_HEADS and 0 < LOOKAHEAD < N_SBUF


def _key_to_float(k):
    bits = k ^ ((k >> 31) & 0x7FFFFFFF)
    return lax.bitcast_convert_type(bits, F32)


def _mid(lo, hi):
    return (lo >> 1) + (hi >> 1) + (lo & hi & 1)


def _tree_sum(parts):
    parts = list(parts)
    while len(parts) > 1:
        nxt = [parts[k] + parts[k + 1] for k in range(0, len(parts) - 1, 2)]
        if len(parts) % 2:
            nxt.append(parts[-1])
        parts = nxt
    return parts[0]


def _bias_body(rb_ref, o_ref, *, offs, sa, sb):
    w = pl.program_id(0)
    h = pl.program_id(1)
    shape = o_ref.shape
    a = lax.broadcasted_iota(I32, shape, 0)
    b = lax.broadcasted_iota(I32, shape, 1)
    off = jnp.where(w == 0, offs[0], offs[1])
    dist = off + sa * a + sb * b
    n = jnp.maximum(dist, 0)
    max_exact = N_BUCKETS // 2
    nf = jnp.maximum(n, 1).astype(F32)
    large = max_exact + jnp.floor(jnp.log(nf / max_exact) / math.log(MAX_DISTANCE / max_exact)
                                  * (N_BUCKETS - max_exact)).astype(I32)
    large = jnp.minimum(large, N_BUCKETS - 1)
    bucket = jnp.where(n < max_exact, n, large)
    val = jnp.zeros(shape, F32)
    for k in range(N_BUCKETS):
        val = jnp.where(bucket == k, rb_ref[k, h], val)
    o_ref[...] = (val - rb_ref[N_BUCKETS - 1, h]) * LOG2E


def _bias_tables(rel_bias, rows, cols, offs, sa, sb):
    return pl.pallas_call(
        functools.partial(_bias_body, offs=offs, sa=sa, sb=sb),
        grid=(2, N_HEADS),
        in_specs=[pl.BlockSpec(memory_space=pltpu.SMEM)],
        out_specs=pl.BlockSpec((None, None, rows, cols), lambda w, h: (w, h, 0, 0)),
        out_shape=jax.ShapeDtypeStruct((2, N_HEADS, rows, cols), F32),
        name="bias_tables",
    )(rel_bias)


def _far_bucket_is_constant(min_dist, max_dist):
    d = np.arange(min_dist, max_dist + 1, dtype=np.float64)
    me = N_BUCKETS // 2
    b = me + np.floor(np.log(d / me) / math.log(MAX_DISTANCE / me) * (N_BUCKETS - me))
    return bool(np.all(np.minimum(b, N_BUCKETS - 1) == N_BUCKETS - 1)) and min_dist >= me


T_Q, T_K, T_V, T_QI, T_KI, T_WI, T_END = 0, 512, 1024, 1536, 1792, 1824, 1840
N_K, N_BG, N_CG, N_H, N_KI, N_END = 0, 512, 1024, 1536, 2048, 2176


def _inproj_body(x_ref, g_ref, wn_ref, wt_ref,
                 qT_ref, kT_ref, vT_ref, vTb_ref, qiT_ref, kiT_ref, wiT_ref,
                 kn_ref, kin_ref, bg_ref, u_ref):
    x = x_ref[...]
    ms = jnp.mean(x * x, axis=-1, keepdims=True)
    xn = ((x * lax.rsqrt(ms + EPS)) * g_ref[...]).astype(BF16)

    def nat(a, b):
        return jnp.dot(xn, wn_ref[:, a:b], preferred_element_type=F32)

    zt = lax.dot_general(wt_ref[...], xn, (((1,), (1,)), ((), ())), preferred_element_type=F32)

    def tra(a, b):
        return zt[a:b, :]

    kn_ref[...] = nat(N_K, N_BG).astype(BF16)
    bg_ref[...] = nat(N_BG, N_CG)
    u_ref[...] = nat(N_CG, N_H) * nat(N_H, N_KI)
    kin_ref[...] = nat(N_KI, N_END).astype(BF16)

    qT_ref[...] = (tra(T_Q, T_K) * (ATTN_SCALE * LOG2E)).astype(BF16)
    kT_ref[...] = tra(T_K, T_V)
    vt = tra(T_V, T_QI)
    vT_ref[...] = vt
    vTb_ref[...] = vt.astype(BF16)
    qiT_ref[...] = tra(T_QI, T_KI).astype(BF16)
    kiT_ref[...] = tra(T_KI, T_WI)
    wiT_ref[...] = tra(T_WI, T_END)[0:N_IDX_HEADS, :] * INDEX_SCALE


def _in_proj(x3, g, wn, wt, tm):
    nb, seq, d = x3.shape
    assert seq % tm == 0
    const = lambda b, i: (0, 0)
    row = lambda b, i: (b, i, 0)
    col = lambda b, i: (b, 0, i)
    di = N_IDX_HEADS * IDX_DIM
    outs = [
        ((nb, D_ATTN, seq), BF16, (None, D_ATTN, tm), col),
        ((nb, D_ATTN, seq), F32, (None, D_ATTN, tm), col),
        ((nb, D_ATTN, seq), F32, (None, D_ATTN, tm), col),
        ((nb, D_ATTN, seq), BF16, (None, D_ATTN, tm), col),
        ((nb, di, seq), BF16, (None, di, tm), col),
        ((nb, IDX_DIM, seq), F32, (None, IDX_DIM, tm), col),
        ((nb, N_IDX_HEADS, seq), F32, (None, N_IDX_HEADS, tm), col),
        ((nb, seq, D_ATTN), BF16, (None, tm, D_ATTN), row),
        ((nb, seq, LANES), BF16, (None, tm, LANES), row),
        ((nb, seq, D_ATTN), F32, (None, tm, D_ATTN), row),
        ((nb, seq, D_ATTN), F32, (None, tm, D_ATTN), row),
    ]
    return pl.pallas_call(
        _inproj_body,
        grid=(nb, seq // tm),
        in_specs=[pl.BlockSpec((None, tm, d), row),
                  pl.BlockSpec((1, d), const),
                  pl.BlockSpec(wn.shape, const),
                  pl.BlockSpec(wt.shape, const)],
        out_specs=[pl.BlockSpec(bs, im) for (_, _, bs, im) in outs],
        out_shape=[jax.ShapeDtypeStruct(s, dt) for (s, dt, _, _) in outs],
        compiler_params=pltpu.CompilerParams(dimension_semantics=("arbitrary", "arbitrary"),
                                             vmem_limit_bytes=VMEM_LIMIT),
        name="in_proj",
    )(x3, g, wn, wt)


def _pattn_body(qT_ref, qiT_ref, wiT_ref, kn_ref, kin_ref, vT_ref, bias_ref, a_ref,
                it_ref, qz_ref, m_ref, l_ref, acc_ref, s_ref,
                *, tq, n_sel):
    i = pl.program_id(1)
    nch = i + 1
    ck = tq
    s_tot = kn_ref.shape[0]
    q_idx = i * tq + lax.broadcasted_iota(I32, (ck, tq), 1)
    row_iota = lax.broadcasted_iota(I32, (ck, tq), 0)

    def idx_dot(j, h):
        r0 = pl.multiple_of(j * ck, ck)
        ki = kin_ref[pl.ds(r0, ck), :][:, 0:IDX_DIM]
        return jnp.dot(ki, qiT_ref[h * IDX_DIM:(h + 1) * IDX_DIM, :],
                       preferred_element_type=F32)

    for h in range(LOOKAHEAD):
        s_ref[h] = idx_dot(0, h)

    def idx_chunk(j, carry):
        r0 = pl.multiple_of(j * ck, ck)
        jn = jnp.minimum(j + 1, nch - 1)
        terms = []
        for h in range(N_IDX_HEADS):
            ahead = h + LOOKAHEAD
            if ahead < N_IDX_HEADS:
                s_ref[ahead] = idx_dot(j, ahead)
            else:
                s_ref[ahead - N_IDX_HEADS] = idx_dot(jn, ahead - N_IDX_HEADS)
            terms.append(wiT_ref[h:h + 1, :] * jnp.maximum(s_ref[h], 0.0))
        acc = _tree_sum(terms)
        acc = jnp.where(r0 + row_iota <= q_idx, acc, -jnp.inf)
        it_ref[pl.ds(r0, ck), :] = acc
        return carry

    lax.fori_loop(0, nch, idx_chunk, 0)

    ck2 = 2 * ck
    row_iota2 = lax.broadcasted_iota(I32, (COUNT_ROWS, tq), 0)

    def count(pred):
        def blocks(r0, n_rows, c8):
            for sb in range(n_rows // COUNT_ROWS):
                base = r0 + sb * COUNT_ROWS
                x = it_ref[pl.ds(base, COUNT_ROWS), :]
                hit = jnp.where(pred(x, base), 1.0, 0.0)
                c8 = c8 + _tree_sum(hit[r * SUBLANES:(r + 1) * SUBLANES]
                                    for r in range(COUNT_ROWS // SUBLANES))
            return c8

        def pair(t, c8):
            return blocks(pl.multiple_of(t * ck2, ck2), ck2, c8)

        def odd(_, c8):
            return blocks(pl.multiple_of((nch - 1) * ck, ck), ck, c8)

        c8 = lax.fori_loop(0, nch >> 1, pair, jnp.zeros((SUBLANES, tq), F32))
        c8 = lax.fori_loop(0, nch & 1, odd, c8)
        return c8.sum(axis=0, keepdims=True)

    def bis(_, st):
        lo, hi, c_lo, c_hi = st
        mid = _mid(lo, hi)
        midf = _key_to_float(mid)
        c = count(lambda x, r0: x >= midf)
        ok = c >= n_sel
        return (jnp.where(ok, mid, lo), jnp.where(ok, hi, mid),
                jnp.where(ok, c, c_lo), jnp.where(ok, c_hi, c))

    zero = jnp.zeros((1, tq), F32)
    lo, _, c_ge, c_gt = lax.fori_loop(
        0, N_BISECT, bis,
        (jnp.full((1, tq), KEY_LO, I32), jnp.full((1, tq), KEY_HI, I32), zero, zero))
    tau = _key_to_float(lo)
    need = n_sel - c_gt

    any_tied = jnp.max(c_ge) > n_sel

    @pl.when(any_tied)
    def _():
        def tie(_, lohi):
            plo, phi = lohi
            pm = (plo + phi) >> 1
            ok = count(lambda x, r0: (x == tau) & (r0 + row_iota2 <= pm)) >= need
            return jnp.where(ok, plo, pm), jnp.where(ok, pm, phi)
        n_it = int(math.ceil(math.log2(s_tot))) + 1
        _, pcut = lax.fori_loop(0, n_it, tie,
                                (jnp.full((1, tq), -1, I32), jnp.full((1, tq), s_tot - 1, I32)))

        def mask_chunk(j, carry):
            r0 = pl.multiple_of(j * ck, ck)
            x = it_ref[pl.ds(r0, ck), :]
            sel = (x > tau) | ((x == tau) & (r0 + row_iota <= pcut))
            it_ref[pl.ds(r0, ck), :] = jnp.where(sel, 0.0, NEG)
            return carry

        lax.fori_loop(0, nch, mask_chunk, 0)

    @pl.when(jnp.logical_not(any_tied))
    def _():
        def mask_chunk(j, carry):
            r0 = pl.multiple_of(j * ck, ck)
            x = it_ref[pl.ds(r0, ck), :]
            it_ref[pl.ds(r0, ck), :] = jnp.where(x >= tau, 0.0, NEG)
            return carry

        lax.fori_loop(0, nch, mask_chunk, 0)

    half = lax.broadcasted_iota(I32, (2 * HEAD_DIM, tq), 0) // HEAD_DIM
    for h in range(N_HEADS):
        pair = qT_ref[(h // 2) * 2 * HEAD_DIM:(h // 2 + 1) * 2 * HEAD_DIM, :]
        qz_ref[h] = jnp.where(half == (h % 2), pair, jnp.zeros_like(pair))
    m_ref[...] = jnp.full(m_ref.shape, NEG, F32)
    l_ref[...] = jnp.zeros(l_ref.shape, F32)
    acc_ref[...] = jnp.zeros(acc_ref.shape, F32)

    def qk(j, h):
        r0 = pl.multiple_of(j * ck, ck)
        kp = kn_ref[pl.ds(r0, ck), (h // 2) * 2 * HEAD_DIM:(h // 2 + 1) * 2 * HEAD_DIM]
        return jnp.dot(kp, qz_ref[h], preferred_element_type=F32) + it_ref[pl.ds(r0, ck), :]

    for h in range(LOOKAHEAD):
        s_ref[h] = qk(0, h)

    def attend(j, bias_w, has_next):
        r0 = pl.multiple_of(j * ck, ck)
        ones = jnp.ones((2 * SUBLANES, ck), BF16)
        for h in range(N_HEADS):
            ahead = h + LOOKAHEAD
            if ahead < N_HEADS:
                s_ref[ahead] = qk(j, ahead)
            elif has_next:
                s_ref[ahead - N_HEADS] = qk(j + 1, ahead - N_HEADS)
            s = s_ref[h]
            if bias_w is not None:
                s = s + bias_ref[bias_w, h]
            m_old = m_ref[h:h + 1, :]
            m_new = jnp.maximum(m_old, jnp.max(s, axis=0, keepdims=True))
            p = jnp.exp2(s - m_new)
            alpha = jnp.exp2(m_old - m_new)
            vt = vT_ref[h * HEAD_DIM:(h + 1) * HEAD_DIM, pl.ds(r0, ck)]
            pv = jnp.dot(jnp.concatenate([vt, ones], axis=0), p.astype(BF16),
                         preferred_element_type=F32)
            l_ref[h:h + 1, :] = alpha * l_ref[h:h + 1, :] + pv[HEAD_DIM:HEAD_DIM + 1, :]
            acc_ref[h * HEAD_DIM:(h + 1) * HEAD_DIM, :] = (
                alpha * acc_ref[h * HEAD_DIM:(h + 1) * HEAD_DIM, :] + pv[0:HEAD_DIM, :])
            m_ref[h:h + 1, :] = m_new

    def far_chunk(j, carry):
        attend(j, None, True)
        return carry

    lax.fori_loop(0, jnp.maximum(i - 1, 0), far_chunk, 0)

    @pl.when(i >= 1)
    def _():
        attend(i - 1, 1, True)

    attend(i, 0, False)

    for h in range(N_HEADS):
        sl = slice(h * HEAD_DIM, (h + 1) * HEAD_DIM)
        acc_ref[sl, :] = acc_ref[sl, :] / l_ref[h:h + 1, :]
    a_ref[...] = acc_ref[...].T.astype(BF16)


def _prompt_attention(qT, qiT, wiT, kn, kin, vTb, bias, tq, n_sel):
    batch, seq, _ = kn.shape
    nq = seq // tq
    blk = lambda b, i: (b, 0, i)
    per_b = lambda b, i: (b, 0, 0)
    return pl.pallas_call(
        functools.partial(_pattn_body, tq=tq, n_sel=n_sel),
        grid=(batch, nq),
        in_specs=[pl.BlockSpec((None, D_ATTN, tq), blk),
                  pl.BlockSpec((None, N_IDX_HEADS * IDX_DIM, tq), blk),
                  pl.BlockSpec((None, N_IDX_HEADS, tq), blk),
                  pl.BlockSpec((None, seq, D_ATTN), per_b),
                  pl.BlockSpec((None, seq, LANES), per_b),
                  pl.BlockSpec((None, D_ATTN, seq), per_b),
                  pl.BlockSpec(bias.shape, lambda b, i: (0, 0, 0, 0))],
        out_specs=pl.BlockSpec((None, tq, D_ATTN), lambda b, i: (b, i, 0)),
        out_shape=jax.ShapeDtypeStruct((batch, seq, D_ATTN), BF16),
        scratch_shapes=[pltpu.VMEM((seq, tq), F32),
                        pltpu.VMEM((N_HEADS, 2 * HEAD_DIM, tq), BF16),
                        pltpu.VMEM((N_HEADS, tq), F32),
                        pltpu.VMEM((N_HEADS, tq), F32),
                        pltpu.VMEM((D_ATTN, tq), F32),
                        pltpu.VMEM((N_SBUF, tq, tq), F32)],
        compiler_params=pltpu.CompilerParams(dimension_semantics=("arbitrary", "arbitrary"),
                                             vmem_limit_bytes=VMEM_LIMIT),
        name="prompt_attention",
    )(qT, qiT, wiT, kn, kin, vTb, bias)


def _sidx_body(pt_ref, qi_ref, w_ref, kinew_ref, *rest, pg, n_pages, t_new, n_sel, group):
    pages = rest[:pg]
    madd_ref = rest[pg]
    it_ref, p_ref = rest[pg + 1:]
    b = pl.program_id(0)
    g = pl.program_id(1)
    ps = pages[0].shape[-1]
    past = n_pages * ps
    tot = past + LANES
    n_slab = tot // LANES
    rows = group * t_new
    rb = pl.multiple_of((b % group) * t_new, t_new)
    qi = qi_ref[...]
    w = w_ref[...]

    def scores(kt):
        s = jnp.dot(qi, kt.astype(BF16), preferred_element_type=F32)
        r = jnp.maximum(s, 0.0) * w
        return _tree_sum(r[h * t_new:(h + 1) * t_new] for h in range(N_IDX_HEADS))

    kt = jnp.concatenate([p[...] for p in pages], axis=-1)
    c0 = pl.multiple_of(g * (pg * ps), pg * ps)
    it_ref[pl.ds(rb, t_new), pl.ds(c0, pg * ps)] = scores(kt)
    last_g = g == pl.num_programs(1) - 1

    @pl.when(last_g)
    def _():
        sn = scores(kinew_ref[...].astype(F32))
        tq_i = lax.broadcasted_iota(I32, (t_new, LANES), 0)
        tk_i = lax.broadcasted_iota(I32, (t_new, LANES), 1)
        it_ref[pl.ds(rb, t_new), past:tot] = jnp.where(tk_i <= tq_i, sn, -jnp.inf)

    @pl.when(last_g & (b % group == group - 1))
    def _():
        lane_i = lax.broadcasted_iota(I32, (rows, LANES), 1)

        def count(pred):
            accs = [jnp.zeros((rows, LANES), F32) for _ in range(2)]
            for sl in range(n_slab):
                x = it_ref[:, sl * LANES:(sl + 1) * LANES]
                accs[sl % 2] = accs[sl % 2] + jnp.where(pred(x, sl * LANES), 1.0, 0.0)
            return jnp.sum(accs[0] + accs[1], axis=1, keepdims=True)

        def bis(_, st):
            lo, hi, c_lo, c_hi = st
            mid = _mid(lo, hi)
            midf = _key_to_float(mid)
            c = count(lambda x, c0_: x >= midf)
            ok = c >= n_sel
            return (jnp.where(ok, mid, lo), jnp.where(ok, hi, mid),
                    jnp.where(ok, c, c_lo), jnp.where(ok, c_hi, c))

        zero = jnp.zeros((rows, 1), F32)
        lo, _, c_ge, c_gt = lax.fori_loop(
            0, N_BISECT, bis,
            (jnp.full((rows, 1), KEY_LO, I32), jnp.full((rows, 1), KEY_HI, I32), zero, zero))
        tau = _key_to_float(lo)
        need = n_sel - c_gt
        p_ref[...] = jnp.full((rows, 1), tot, I32)

        @pl.when(jnp.max(c_ge) > n_sel)
        def _():
            def tie(_, lohi):
                plo, phi = lohi
                pm = (plo + phi) >> 1
                ok = count(lambda x, c0_: (x == tau) & (c0_ + lane_i <= pm)) >= need
                return jnp.where(ok, plo, pm), jnp.where(ok, pm, phi)
            n_it = int(math.ceil(math.log2(tot))) + 1
            _, phi = lax.fori_loop(0, n_it, tie, (jnp.full((rows, 1), -1, I32),
                                                  jnp.full((rows, 1), tot - 1, I32)))
            p_ref[...] = phi

        pcut = p_ref[...]
        for sl in range(n_slab):
            x = it_ref[:, sl * LANES:(sl + 1) * LANES]
            sel = (x > tau) | ((x == tau) & (sl * LANES + lane_i <= pcut))
            madd_ref[:, :, sl * LANES:(sl + 1) * LANES] = jnp.where(sel, 0.0, NEG).reshape(
                group, t_new, LANES)


def _sample_select(page_table, qi_s, w_s, kinew, kidxT, pg, n_sel, group):
    db, n_pages = page_table.shape
    ps = kidxT.shape[-1]
    t_new = qi_s.shape[1] // N_IDX_HEADS
    tot = n_pages * ps + LANES
    assert n_pages % pg == 0 and ps == LANES and db % group == 0

    def page_spec(u):
        return pl.BlockSpec((None, IDX_DIM, ps), lambda b, g, pt: (pt[b, g * pg + u], 0, 0))

    grid_spec = pltpu.PrefetchScalarGridSpec(
        num_scalar_prefetch=1,
        grid=(db, n_pages // pg),
        in_specs=[pl.BlockSpec((None,) + qi_s.shape[1:], lambda b, g, pt: (b, 0, 0)),
                  pl.BlockSpec((None,) + w_s.shape[1:], lambda b, g, pt: (b, 0, 0)),
                  pl.BlockSpec((None,) + kinew.shape[1:], lambda b, g, pt: (b, 0, 0))]
                 + [page_spec(u) for u in range(pg)],
        out_specs=pl.BlockSpec((group, t_new, tot), lambda b, g, pt: (b // group, 0, 0)),
        scratch_shapes=[pltpu.VMEM((group * t_new, tot), F32),
                        pltpu.VMEM((group * t_new, 1), I32)])
    return pl.pallas_call(
        functools.partial(_sidx_body, pg=pg, n_pages=n_pages, t_new=t_new, n_sel=n_sel,
                          group=group),
        grid_spec=grid_spec,
        out_shape=jax.ShapeDtypeStruct((db, t_new, tot), F32),
        compiler_params=pltpu.CompilerParams(dimension_semantics=("arbitrary", "arbitrary"),
                                             vmem_limit_bytes=VMEM_LIMIT),
        name="sample_select",
    )(page_table, qi_s, w_s, kinew, *([kidxT] * pg))


def _sattn_body(pt_ref, qbd_ref, madd_ref, maddn_ref, knew_ref, vnew_ref, bias_ref, *rest, pg):
    kpages = rest[:pg]
    vpages = rest[pg:2 * pg]
    o_ref = rest[2 * pg]
    m_ref, l_ref, acc_ref = rest[2 * pg + 1:]
    g = pl.program_id(1)
    last = pl.num_programs(1) - 1
    qbd = qbd_ref[...]
    ps = kpages[0].shape[-1]
    t_new = madd_ref.shape[0]
    hd = N_HEADS * HEAD_DIM

    @pl.when(g == 0)
    def _():
        m_ref[...] = jnp.full(m_ref.shape, NEG, F32)
        l_ref[...] = jnp.zeros(l_ref.shape, F32)
        acc_ref[...] = jnp.zeros(acc_ref.shape, F32)

    def flash(s, vt):
        m_old = m_ref[...]
        m_new = jnp.maximum(m_old, jnp.max(s, axis=-1, keepdims=True))
        p = jnp.exp2(s - m_new)
        alpha = jnp.exp2(m_old - m_new)
        l_ref[...] = alpha * l_ref[...] + jnp.sum(p, axis=-1, keepdims=True)
        pv = lax.dot_general(p.astype(BF16), vt, (((1,), (1,)), ((), ())),
                             preferred_element_type=F32)
        acc_ref[...] = alpha * acc_ref[...] + pv
        m_ref[...] = m_new

    def add_rows(s, add):
        n = s.shape[-1]
        return (s.reshape(N_HEADS, t_new, n) + add[None]).reshape(N_HEADS * t_new, n)

    kt = jnp.concatenate([kp[...].reshape(hd, ps) for kp in kpages], axis=-1).astype(BF16)
    vt = jnp.concatenate([vp[...].reshape(hd, ps) for vp in vpages], axis=-1).astype(BF16)
    s = jnp.dot(qbd, kt, preferred_element_type=F32)
    s = add_rows(s, madd_ref[...])
    is_last = jnp.where(g == last, 1.0, 0.0)
    tail = s[:, (pg - 1) * ps:] + is_last * bias_ref[0].reshape(N_HEADS * t_new, ps)
    s = jnp.concatenate([s[:, :(pg - 1) * ps], tail], axis=-1)
    flash(s, vt)

    @pl.when(g == last)
    def _():
        sn = jnp.dot(qbd, knew_ref[...], preferred_element_type=F32)
        sn = add_rows(sn + bias_ref[1].reshape(N_HEADS * t_new, LANES), maddn_ref[...])
        flash(sn, vnew_ref[...])
        out = acc_ref[...] / l_ref[...]
        for h in range(N_HEADS):
            o_ref[h] = out[h * t_new:(h + 1) * t_new, h * HEAD_DIM:(h + 1) * HEAD_DIM]


def _sample_attention(page_table, qbd, madd, knew, vnew, bias_s, cache_kT, cache_vT, pg):
    db, n_pages = page_table.shape
    ps = cache_kT.shape[-1]
    t_new = madd.shape[1]
    assert n_pages % pg == 0

    def page_spec(u):
        return pl.BlockSpec((None, N_HEADS, HEAD_DIM, ps),
                            lambda b, g, pt: (pt[b, g * pg + u], 0, 0, 0))

    per_b = lambda b, g, pt: (b, 0, 0)
    grid_spec = pltpu.PrefetchScalarGridSpec(
        num_scalar_prefetch=1,
        grid=(db, n_pages // pg),
        in_specs=[pl.BlockSpec((None,) + qbd.shape[1:], per_b),
                  pl.BlockSpec((None, t_new, pg * ps), lambda b, g, pt: (b, 0, g)),
                  pl.BlockSpec((None, t_new, LANES), lambda b, g, pt: (b, 0, n_pages * ps // LANES)),
                  pl.BlockSpec((None,) + knew.shape[1:], per_b),
                  pl.BlockSpec((None,) + vnew.shape[1:], per_b),
                  pl.BlockSpec(bias_s.shape, lambda b, g, pt: (0, 0, 0, 0))]
                 + [page_spec(u) for u in range(pg)] * 2,
        out_specs=pl.BlockSpec((None, N_HEADS, t_new, HEAD_DIM), lambda b, g, pt: (b, 0, 0, 0)),
        scratch_shapes=[pltpu.VMEM((N_HEADS * t_new, 1), F32),
                        pltpu.VMEM((N_HEADS * t_new, 1), F32),
                        pltpu.VMEM((N_HEADS * t_new, N_HEADS * HEAD_DIM), F32)])
    return pl.pallas_call(
        functools.partial(_sattn_body, pg=pg),
        grid_spec=grid_spec,
        out_shape=jax.ShapeDtypeStruct((db, N_HEADS, t_new, HEAD_DIM), F32),
        compiler_params=pltpu.CompilerParams(dimension_semantics=("arbitrary", "arbitrary"),
                                             vmem_limit_bytes=VMEM_LIMIT),
        name="sample_attention",
    )(page_table, qbd, madd, madd, knew, vnew, bias_s, *([cache_kT] * pg), *([cache_vT] * pg))


def _outmlp_body(x_ref, a_ref, bg_ref, u_ref, prev_ref, cw_ref, wo_ref, gm_ref, wu_ref, wd_ref,
                 gf_ref, y_ref, *, seq_len, ff_chunk):
    tm = x_ref.shape[0]
    u = u_ref[...]
    w0 = cw_ref[0:1, :]
    w1 = cw_ref[1:2, :]
    w2 = cw_ref[2:3, :]
    if seq_len >= tm:
        first = (pl.program_id(0) % (seq_len // tm)) == 0
        halo = prev_ref[...] * jnp.where(first, 0.0, 1.0)
        row = lax.broadcasted_iota(I32, u.shape, 0)
        um1 = jnp.where(row == 0, halo[7:8, :], pltpu.roll(u, 1, 0))
        um2 = jnp.where(row == 0, halo[6:7, :],
                        jnp.where(row == 1, halo[7:8, :], pltpu.roll(u, 2, 0)))
    else:
        nseq = tm // seq_len
        u3 = u.reshape(nseq, seq_len, u.shape[-1])
        up = jnp.concatenate([prev_ref[...], u3], axis=1)
        um1 = up[:, 1:1 + seq_len].reshape(u.shape)
        um2 = up[:, 0:seq_len].reshape(u.shape)
    y = um2 * w0
    y = y + um1 * w1
    y = y + u * w2
    b = (bg_ref[...] * y).astype(BF16)
    ab = jnp.concatenate([a_ref[...], b], axis=-1)
    x1 = x_ref[...] + jnp.dot(ab, wo_ref[...], preferred_element_type=F32)
    ms = jnp.mean(x1 * x1, axis=-1, keepdims=True)
    hn = ((x1 * lax.rsqrt(ms + EPS)) * gm_ref[...]).astype(BF16)
    acc = jnp.zeros(x1.shape, F32)
    d_ff = wu_ref.shape[1]
    for c in range(d_ff // ff_chunk):
        sl = slice(c * ff_chunk, (c + 1) * ff_chunk)
        up_c = jnp.dot(hn, wu_ref[:, sl], preferred_element_type=F32)
        r = jnp.maximum(up_c, 0.0)
        acc = acc + jnp.dot((r * r).astype(BF16), wd_ref[sl, :], preferred_element_type=F32)
    x2 = x1 + acc
    ms2 = jnp.mean(x2 * x2, axis=-1, keepdims=True)
    y_ref[...] = (x2 * lax.rsqrt(ms2 + EPS)) * gf_ref[...]


def _out_mlp(x2d, a, bg, u, prev, conv_w, wo, g_mlp, wu, wd, g_final, tm, seq_len):
    n, d = x2d.shape
    dc = bg.shape[1]
    row = lambda i: (i, 0)
    c2 = lambda i: (0, 0)
    if seq_len >= tm:
        prev_spec = pl.BlockSpec((SUBLANES, dc),
                                 lambda i: (jnp.maximum(i * (tm // SUBLANES) - 1, 0), 0))
        prev_arg = u
    else:
        nseq = tm // seq_len
        prev_spec = pl.BlockSpec((nseq,) + prev.shape[1:], lambda i: (i, 0, 0))
        prev_arg = prev
    single = dict(pipeline_mode=pl.Buffered(1))
    return pl.pallas_call(
        functools.partial(_outmlp_body, seq_len=seq_len, ff_chunk=1024),
        grid=(n // tm,),
        in_specs=[pl.BlockSpec((tm, d), row),
                  pl.BlockSpec((tm, a.shape[1]), row),
                  pl.BlockSpec((tm, dc), row),
                  pl.BlockSpec((tm, dc), row),
                  prev_spec,
                  pl.BlockSpec(conv_w.shape, c2),
                  pl.BlockSpec(wo.shape, c2, **single),
                  pl.BlockSpec((1, d), c2),
                  pl.BlockSpec(wu.shape, c2, **single),
                  pl.BlockSpec(wd.shape, c2, **single),
                  pl.BlockSpec((1, d), c2)],
        out_specs=pl.BlockSpec((tm, d), row),
        out_shape=jax.ShapeDtypeStruct((n, d), F32),
        compiler_params=pltpu.CompilerParams(dimension_semantics=("arbitrary",),
                                             vmem_limit_bytes=VMEM_LIMIT),
        name="out_mlp",
    )(x2d, a, bg, u, prev_arg, conv_w, wo, g_mlp, wu, wd, g_final)


def _pick_tile(n, pref):
    t = min(pref, n)
    while n % t:
        t //= 2
    return t


def kernel(x_prompt, x_sample, cache_k, cache_v, cache_kidx, state_conv, page_table, rel_bias,
           g_mix, w_in, conv_w, w_out, g_mlp, w_up, w_down, g_final):
    depth = w_in.shape[0]
    assert depth == 1, "single-layer step"
    batch, seq, d_model = x_prompt.shape
    db, t_new, _ = x_sample.shape
    n_pages = page_table.shape[1]
    ps = cache_k.shape[2]
    past = n_pages * ps
    assert ps == LANES and t_new == SUBLANES

    tq = _pick_tile(seq, 256)
    n_sel_p = min(TOPK_MAX, seq // 4)
    n_sel_s = min(TOPK_MAX, (past + t_new) // 4)
    assert _far_bucket_is_constant(tq + 1, seq) and _far_bucket_is_constant(ps + 1, past + t_new)

    w = w_in[0]
    cq, ck_, cv, cqi, cki, cwi, cbg, ccg, ch = np.cumsum(
        [0, D_ATTN, D_ATTN, D_ATTN, N_IDX_HEADS * IDX_DIM, IDX_DIM, N_IDX_HEADS, D_ATTN, D_ATTN])
    end = ch + D_ATTN
    wt = jnp.pad(w[:, cq:cbg].T, ((0, T_END - cbg), (0, 0))).astype(BF16)
    wn = jnp.concatenate(
        [w[:, ck_:cv], w[:, cbg:end], jnp.pad(w[:, cki:cwi], ((0, 0), (0, LANES - IDX_DIM)))],
        axis=1).astype(BF16)
    wo = w_out[0].astype(BF16)
    wu = w_up[0].astype(BF16)
    wd = w_down[0].astype(BF16)
    gmix = g_mix[0][None]
    gmlp = g_mlp[0][None]
    gfin = g_final[None]
    cw = conv_w[0]

    n_p = batch * seq
    (qT, kT, vT, vTb, qiT, kiT, wiT, kn, kin, bg, u) = _in_proj(
        x_prompt, gmix, wn, wt, _pick_tile(seq, 512))
    bias_p = _bias_tables(rel_bias, tq, tq, (0, tq), -1, 1)
    a_p = _prompt_attention(qT, qiT, wiT, kn, kin, vTb, bias_p, tq, n_sel_p)
    y_prompt = _out_mlp(x_prompt.reshape(n_p, d_model), a_p.reshape(n_p, D_ATTN),
                        bg.reshape(n_p, D_ATTN), u.reshape(n_p, D_ATTN), None, cw, wo, gmlp, wu, wd,
                        gfin, _pick_tile(seq, 512), seq).reshape(batch, seq, d_model)

    def heads_out(t):
        b_, _, s_ = t.shape
        return t.reshape(b_, N_HEADS, HEAD_DIM, s_).transpose(0, 3, 1, 2)[None]

    k_prompt = heads_out(kT)
    v_prompt = heads_out(vT)
    kidx_prompt = kiT.transpose(0, 2, 1)[None]
    conv_prompt = u[:, seq - (CONV_WIDTH - 1):][None]

    ns = db * t_new
    (qTs, kTs, vTs, _, qiTs, kiTs, wiTs, _, _, bgs, us) = [
        t[0] for t in _in_proj(x_sample.reshape(1, ns, d_model), gmix, wn, wt, ns)]
    qi_s = qiTs.reshape(N_IDX_HEADS, IDX_DIM, db, t_new).transpose(2, 0, 3, 1).reshape(
        db, N_IDX_HEADS * t_new, IDX_DIM)
    w_s = wiTs.reshape(N_IDX_HEADS, db, t_new).transpose(1, 0, 2).reshape(db, N_IDX_HEADS * t_new, 1)
    kinew = jnp.pad(kiTs.reshape(IDX_DIM, db, t_new).transpose(1, 0, 2),
                    ((0, 0), (0, 0), (0, LANES - t_new))).astype(BF16)
    q_s = qTs.reshape(N_HEADS, HEAD_DIM, db, t_new).transpose(2, 0, 3, 1)
    eye = jnp.eye(N_HEADS, dtype=q_s.dtype)
    qbd = (q_s[:, :, :, None, :] * eye[None, :, None, :, None]).reshape(
        db, N_HEADS * t_new, N_HEADS * HEAD_DIM)
    pad_new = lambda t: jnp.pad(t.reshape(D_ATTN, db, t_new).transpose(1, 0, 2),
                                ((0, 0), (0, 0), (0, LANES - t_new))).astype(BF16)
    knew = pad_new(kTs)
    vnew = pad_new(vTs)
    kidxT = cache_kidx[0].transpose(0, 2, 1)
    cache_kT = cache_k[0].transpose(0, 2, 3, 1)
    cache_vT = cache_v[0].transpose(0, 2, 3, 1)

    madd = _sample_select(page_table, qi_s, w_s, kinew, kidxT, _pick_tile(n_pages, 128), n_sel_s,
                          _pick_tile(db, 4))
    bias_s = _bias_tables(rel_bias, t_new, LANES, (ps, 0), 1, -1)
    o_s = _sample_attention(page_table, qbd, madd, knew, vnew, bias_s, cache_kT, cache_vT,
                            _pick_tile(n_pages, 32))
    a_s = o_s.transpose(0, 2, 1, 3).reshape(ns, D_ATTN).astype(BF16)
    y_sample = _out_mlp(x_sample.reshape(ns, d_model), a_s, bgs, us, state_conv[0], cw, wo, gmlp,
                        wu, wd, gfin, ns, t_new).reshape(db, t_new, d_model)

    def heads_out_s(t):
        return t.reshape(N_HEADS, HEAD_DIM, db, t_new).transpose(2, 3, 0, 1)[None]

    k_sample = heads_out_s(kTs)
    v_sample = heads_out_s(vTs)
    kidx_sample = kiTs.reshape(IDX_DIM, db, t_new).transpose(1, 2, 0)[None]
    conv_sample = us.reshape(db, t_new, D_ATTN)[:, t_new - (CONV_WIDTH - 1):][None]

    return (y_prompt, y_sample, k_prompt, v_prompt, kidx_prompt, conv_prompt,
            k_sample, v_sample, kidx_sample, conv_sample)
```

```python
import functools
import math

import jax
import jax.numpy as jnp
import numpy as np
from jax import lax
from jax.experimental import pallas as pl
from jax.experimental.pallas import tpu as pltpu

F32 = jnp.float32
BF16 = jnp.bfloat16
I32 = jnp.int32

HEAD_DIM = 64
N_HEADS = 8
N_IDX_HEADS = 8
IDX_DIM = 32
D_ATTN = N_HEADS * HEAD_DIM
TOPK_MAX = 256
CONV_WIDTH = 3
N_BUCKETS = 32
MAX_DISTANCE = 128
EPS = 1e-6
ATTN_SCALE = HEAD_DIM ** -0.5
INDEX_SCALE = (IDX_DIM ** -0.5) * (N_IDX_HEADS ** -0.5)
LOG2E = math.log2(math.e)

LANES = 128
SUBLANES = 8
NEG = -1e30
VMEM_LIMIT = 56 * 1024 * 1024


def _float_key(v):
    b = int(np.array(v, np.float32).view(np.int32))
    return b if b >= 0 else b ^ 0x7FFFFFFF


KEY_LO = _float_key(-np.finfo(np.float32).max)
KEY_HI = _float_key(np.inf)
N_BISECT = 32
assert KEY_HI - KEY_LO < 1 << N_BISECT
COUNT_ROWS = 64
N_SBUF = N_HEADS
LOOKAHEAD = 5
assert N_IDX_HEADS == N_HEADS and 0 < LOOKAHEAD < N_SBUF


def _key_to_float(k):
    bits = k ^ ((k >> 31) & 0x7FFFFFFF)
    return lax.bitcast_convert_type(bits, F32)


def _mid(lo, hi):
    return (lo >> 1) + (hi >> 1) + (lo & hi & 1)


def _tree_sum(parts):
    parts = list(parts)
    while len(parts) > 1:
        nxt = [parts[k] + parts[k + 1] for k in range(0, len(parts) - 1, 2)]
        if len(parts) % 2:
            nxt.append(parts[-1])
        parts = nxt
    return parts[0]


def _bias_body(rb_ref, o_ref, *, offs, sa, sb):
    w = pl.program_id(0)
    h = pl.program_id(1)
    shape = o_ref.shape
    a = lax.broadcasted_iota(I32, shape, 0)
    b = lax.broadcasted_iota(I32, shape, 1)
    off = jnp.where(w == 0, offs[0], offs[1])
    dist = off + sa * a + sb * b
    n = jnp.maximum(dist, 0)
    max_exact = N_BUCKETS // 2
    nf = jnp.maximum(n, 1).astype(F32)
    large = max_exact + jnp.floor(jnp.log(nf / max_exact) / math.log(MAX_DISTANCE / max_exact)
                                  * (N_BUCKETS - max_exact)).astype(I32)
    large = jnp.minimum(large, N_BUCKETS - 1)
    bucket = jnp.where(n < max_exact, n, large)
    val = jnp.zeros(shape, F32)
    for k in range(N_BUCKETS):
        val = jnp.where(bucket == k, rb_ref[k, h], val)
    o_ref[...] = (val - rb_ref[N_BUCKETS - 1, h]) * LOG2E


def _bias_tables(rel_bias, rows, cols, offs, sa, sb):
    return pl.pallas_call(
        functools.partial(_bias_body, offs=offs, sa=sa, sb=sb),
        grid=(2, N_HEADS),
        in_specs=[pl.BlockSpec(memory_space=pltpu.SMEM)],
        out_specs=pl.BlockSpec((None, None, rows, cols), lambda w, h: (w, h, 0, 0)),
        out_shape=jax.ShapeDtypeStruct((2, N_HEADS, rows, cols), F32),
        name="bias_tables",
    )(rel_bias)


def _far_bucket_is_constant(min_dist, max_dist):
    d = np.arange(min_dist, max_dist + 1, dtype=np.float64)
    me = N_BUCKETS // 2
    b = me + np.floor(np.log(d / me) / math.log(MAX_DISTANCE / me) * (N_BUCKETS - me))
    return bool(np.all(np.minimum(b, N_BUCKETS - 1) == N_BUCKETS - 1)) and min_dist >= me


T_Q, T_K, T_V, T_QI, T_KI, T_WI, T_END = 0, 512, 1024, 1536, 1792, 1824, 1840
N_K, N_BG, N_CG, N_H, N_KI, N_END = 0, 512, 1024, 1536, 2048, 2176


def _inproj_body(x_ref, g_ref, wn_ref, wt_ref,
                 qT_ref, kT_ref, vT_ref, vTb_ref, qiT_ref, kiT_ref, wiT_ref,
                 kn_ref, kin_ref, bg_ref, u_ref):
    x = x_ref[...]
    ms = jnp.mean(x * x, axis=-1, keepdims=True)
    xn = ((x * lax.rsqrt(ms + EPS)) * g_ref[...]).astype(BF16)

    def nat(a, b):
        return jnp.dot(xn, wn_ref[:, a:b], preferred_element_type=F32)

    zt = lax.dot_general(wt_ref[...], xn, (((1,), (1,)), ((), ())), preferred_element_type=F32)

    def tra(a, b):
        return zt[a:b, :]

    kn_ref[...] = nat(N_K, N_BG).astype(BF16)
    bg_ref[...] = nat(N_BG, N_CG)
    u_ref[...] = nat(N_CG, N_H) * nat(N_H, N_KI)
    kin_ref[...] = nat(N_KI, N_END).astype(BF16)

    qT_ref[...] = (tra(T_Q, T_K) * (ATTN_SCALE * LOG2E)).astype(BF16)
    kT_ref[...] = tra(T_K, T_V)
    vt = tra(T_V, T_QI)
    vT_ref[...] = vt
    vTb_ref[...] = vt.astype(BF16)
    qiT_ref[...] = tra(T_QI, T_KI).astype(BF16)
    kiT_ref[...] = tra(T_KI, T_WI)
    wiT_ref[...] = tra(T_WI, T_END)[0:N_IDX_HEADS, :] * INDEX_SCALE


def _in_proj(x3, g, wn, wt, tm):
    nb, seq, d = x3.shape
    assert seq % tm == 0
    const = lambda b, i: (0, 0)
    row = lambda b, i: (b, i, 0)
    col = lambda b, i: (b, 0, i)
    di = N_IDX_HEADS * IDX_DIM
    outs = [
        ((nb, D_ATTN, seq), BF16, (None, D_ATTN, tm), col),
        ((nb, D_ATTN, seq), F32, (None, D_ATTN, tm), col),
        ((nb, D_ATTN, seq), F32, (None, D_ATTN, tm), col),
        ((nb, D_ATTN, seq), BF16, (None, D_ATTN, tm), col),
        ((nb, di, seq), BF16, (None, di, tm), col),
        ((nb, IDX_DIM, seq), F32, (None, IDX_DIM, tm), col),
        ((nb, N_IDX_HEADS, seq), F32, (None, N_IDX_HEADS, tm), col),
        ((nb, seq, D_ATTN), BF16, (None, tm, D_ATTN), row),
        ((nb, seq, LANES), BF16, (None, tm, LANES), row),
        ((nb, seq, D_ATTN), F32, (None, tm, D_ATTN), row),
        ((nb, seq, D_ATTN), F32, (None, tm, D_ATTN), row),
    ]
    return pl.pallas_call(
        _inproj_body,
        grid=(nb, seq // tm),
        in_specs=[pl.BlockSpec((None, tm, d), row),
                  pl.BlockSpec((1, d), const),
                  pl.BlockSpec(wn.shape, const),
                  pl.BlockSpec(wt.shape, const)],
        out_specs=[pl.BlockSpec(bs, im) for (_, _, bs, im) in outs],
        out_shape=[jax.ShapeDtypeStruct(s, dt) for (s, dt, _, _) in outs],
        compiler_params=pltpu.CompilerParams(dimension_semantics=("arbitrary", "arbitrary"),
                                             vmem_limit_bytes=VMEM_LIMIT),
        name="in_proj",
    )(x3, g, wn, wt)


def _pattn_body(qT_ref, qiT_ref, wiT_ref, kn_ref, kin_ref, vT_ref, bias_ref, a_ref,
                it_ref, qz_ref, m_ref, l_ref, acc_ref, s_ref,
                *, tq, n_sel):
    i = pl.program_id(1)
    nch = i + 1
    ck = tq
    s_tot = kn_ref.shape[0]
    q_idx = i * tq + lax.broadcasted_iota(I32, (ck, tq), 1)
    row_iota = lax.broadcasted_iota(I32, (ck, tq), 0)

    def idx_dot(j, h):
        r0 = pl.multiple_of(j * ck, ck)
        ki = kin_ref[pl.ds(r0, ck), :][:, 0:IDX_DIM]
        return jnp.dot(ki, qiT_ref[h * IDX_DIM:(h + 1) * IDX_DIM, :],
                       preferred_element_type=F32)

    for h in range(LOOKAHEAD):
        s_ref[h] = idx_dot(0, h)

    def idx_chunk(j, carry):
        r0 = pl.multiple_of(j * ck, ck)
        jn = jnp.minimum(j + 1, nch - 1)
        terms = []
        for h in range(N_IDX_HEADS):
            ahead = h + LOOKAHEAD
            if ahead < N_IDX_HEADS:
                s_ref[ahead] = idx_dot(j, ahead)
            else:
                s_ref[ahead - N_IDX_HEADS] = idx_dot(jn, ahead - N_IDX_HEADS)
            terms.append(wiT_ref[h:h + 1, :] * jnp.maximum(s_ref[h], 0.0))
        acc = _tree_sum(terms)
        acc = jnp.where(r0 + row_iota <= q_idx, acc, -jnp.inf)
        it_ref[pl.ds(r0, ck), :] = acc
        return carry

    lax.fori_loop(0, nch, idx_chunk, 0)

    ck2 = 2 * ck
    row_iota2 = lax.broadcasted_iota(I32, (COUNT_ROWS, tq), 0)

    def count(pred):
        def blocks(r0, n_rows, c8):
            for sb in range(n_rows // COUNT_ROWS):
                base = r0 + sb * COUNT_ROWS
                x = it_ref[pl.ds(base, COUNT_ROWS), :]
                hit = jnp.where(pred(x, base), 1.0, 0.0)
                c8 = c8 + _tree_sum(hit[r * SUBLANES:(r + 1) * SUBLANES]
                                    for r in range(COUNT_ROWS // SUBLANES))
            return c8

        def pair(t, c8):
            return blocks(pl.multiple_of(t * ck2, ck2), ck2, c8)

        def odd(_, c8):
            return blocks(pl.multiple_of((nch - 1) * ck, ck), ck, c8)

        c8 = lax.fori_loop(0, nch >> 1, pair, jnp.zeros((SUBLANES, tq), F32))
        c8 = lax.fori_loop(0, nch & 1, odd, c8)
        return c8.sum(axis=0, keepdims=True)

    def bis(_, st):
        lo, hi, c_lo, c_hi = st
        mid = _mid(lo, hi)
        midf = _key_to_float(mid)
        c = count(lambda x, r0: x >= midf)
        ok = c >= n_sel
        return (jnp.where(ok, mid, lo), jnp.where(ok, hi, mid),
                jnp.where(ok, c, c_lo), jnp.where(ok, c_hi, c))

    zero = jnp.zeros((1, tq), F32)
    n_pass = jnp.where((i + 1) * tq <= n_sel, 0, N_BISECT)
    lo, _, c_ge, c_gt = lax.fori_loop(
        0, n_pass, bis,
        (jnp.full((1, tq), KEY_LO, I32), jnp.full((1, tq), KEY_HI, I32), zero, zero))
    tau = _key_to_float(lo)
    need = n_sel - c_gt

    any_tied = jnp.max(c_ge) > n_sel

    @pl.when(any_tied)
    def _():
        def tie(_, lohi):
            plo, phi = lohi
            pm = (plo + phi) >> 1
            ok = count(lambda x, r0: (x == tau) & (r0 + row_iota2 <= pm)) >= need
            return jnp.where(ok, plo, pm), jnp.where(ok, pm, phi)
        n_it = int(math.ceil(math.log2(s_tot))) + 1
        _, pcut = lax.fori_loop(0, n_it, tie,
                                (jnp.full((1, tq), -1, I32), jnp.full((1, tq), s_tot - 1, I32)))

        def mask_chunk(j, carry):
            r0 = pl.multiple_of(j * ck, ck)
            x = it_ref[pl.ds(r0, ck), :]
            sel = (x > tau) | ((x == tau) & (r0 + row_iota <= pcut))
            it_ref[pl.ds(r0, ck), :] = jnp.where(sel, 0.0, NEG)
            return carry

        lax.fori_loop(0, nch, mask_chunk, 0)

    @pl.when(jnp.logical_not(any_tied))
    def _():
        def mask_chunk(j, carry):
            r0 = pl.multiple_of(j * ck, ck)
            x = it_ref[pl.ds(r0, ck), :]
            it_ref[pl.ds(r0, ck), :] = jnp.where(x >= tau, 0.0, NEG)
            return carry

        lax.fori_loop(0, nch, mask_chunk, 0)

    half = lax.broadcasted_iota(I32, (2 * HEAD_DIM, tq), 0) // HEAD_DIM
    for h in range(N_HEADS):
        pair = qT_ref[(h // 2) * 2 * HEAD_DIM:(h // 2 + 1) * 2 * HEAD_DIM, :]
        qz_ref[h] = jnp.where(half == (h % 2), pair, jnp.zeros_like(pair))
    m_ref[...] = jnp.full(m_ref.shape, NEG, F32)
    l_ref[...] = jnp.zeros(l_ref.shape, F32)
    acc_ref[...] = jnp.zeros(acc_ref.shape, F32)

    def qk(j, h):
        r0 = pl.multiple_of(j * ck, ck)
        kp = kn_ref[pl.ds(r0, ck), (h // 2) * 2 * HEAD_DIM:(h // 2 + 1) * 2 * HEAD_DIM]
        return jnp.dot(kp, qz_ref[h], preferred_element_type=F32) + it_ref[pl.ds(r0, ck), :]

    for h in range(LOOKAHEAD):
        s_ref[h] = qk(0, h)

    def attend(j, bias_w, has_next):
        r0 = pl.multiple_of(j * ck, ck)
        ones = jnp.ones((2 * SUBLANES, ck), BF16)
        for h in range(N_HEADS):
            ahead = h + LOOKAHEAD
            if ahead < N_HEADS:
                s_ref[ahead] = qk(j, ahead)
            elif has_next:
                s_ref[ahead - N_HEADS] = qk(j + 1, ahead - N_HEADS)
            s = s_ref[h]
            if bias_w is not None:
                s = s + bias_ref[bias_w, h]
            m_old = m_ref[h:h + 1, :]
            m_new = jnp.maximum(m_old, jnp.max(s, axis=0, keepdims=True))
            p = jnp.exp2(s - m_new)
            alpha = jnp.exp2(m_old - m_new)
            vt = vT_ref[h * HEAD_DIM:(h + 1) * HEAD_DIM, pl.ds(r0, ck)]
            pv = jnp.dot(jnp.concatenate([vt, ones], axis=0), p.astype(BF16),
                         preferred_element_type=F32)
            l_ref[h:h + 1, :] = alpha * l_ref[h:h + 1, :] + pv[HEAD_DIM:HEAD_DIM + 1, :]
            acc_ref[h * HEAD_DIM:(h + 1) * HEAD_DIM, :] = (
                alpha * acc_ref[h * HEAD_DIM:(h + 1) * HEAD_DIM, :] + pv[0:HEAD_DIM, :])
            m_ref[h:h + 1, :] = m_new

    def far_chunk(j, carry):
        attend(j, None, True)
        return carry

    lax.fori_loop(0, jnp.maximum(i - 1, 0), far_chunk, 0)

    @pl.when(i >= 1)
    def _():
        attend(i - 1, 1, True)

    attend(i, 0, False)

    for h in range(N_HEADS):
        sl = slice(h * HEAD_DIM, (h + 1) * HEAD_DIM)
        acc_ref[sl, :] = acc_ref[sl, :] / l_ref[h:h + 1, :]
    a_ref[...] = acc_ref[...].T.astype(BF16)


def _prompt_attention(qT, qiT, wiT, kn, kin, vTb, bias, tq, n_sel):
    batch, seq, _ = kn.shape
    nq = seq // tq
    blk = lambda b, i: (b, 0, i)
    per_b = lambda b, i: (b, 0, 0)
    return pl.pallas_call(
        functools.partial(_pattn_body, tq=tq, n_sel=n_sel),
        grid=(batch, nq),
        in_specs=[pl.BlockSpec((None, D_ATTN, tq), blk),
                  pl.BlockSpec((None, N_IDX_HEADS * IDX_DIM, tq), blk),
                  pl.BlockSpec((None, N_IDX_HEADS, tq), blk),
                  pl.BlockSpec((None, seq, D_ATTN), per_b),
                  pl.BlockSpec((None, seq, LANES), per_b),
                  pl.BlockSpec((None, D_ATTN, seq), per_b),
                  pl.BlockSpec(bias.shape, lambda b, i: (0, 0, 0, 0))],
        out_specs=pl.BlockSpec((None, tq, D_ATTN), lambda b, i: (b, i, 0)),
        out_shape=jax.ShapeDtypeStruct((batch, seq, D_ATTN), BF16),
        scratch_shapes=[pltpu.VMEM((seq, tq), F32),
                        pltpu.VMEM((N_HEADS, 2 * HEAD_DIM, tq), BF16),
                        pltpu.VMEM((N_HEADS, tq), F32),
                        pltpu.VMEM((N_HEADS, tq), F32),
                        pltpu.VMEM((D_ATTN, tq), F32),
                        pltpu.VMEM((N_SBUF, tq, tq), F32)],
        compiler_params=pltpu.CompilerParams(dimension_semantics=("arbitrary", "arbitrary"),
                                             vmem_limit_bytes=VMEM_LIMIT),
        name="prompt_attention",
    )(qT, qiT, wiT, kn, kin, vTb, bias)


def _sidx_body(pt_ref, qi_ref, w_ref, kinew_ref, *rest, pg, n_pages, t_new, n_sel, group):
    pages = rest[:pg]
    madd_ref = rest[pg]
    it_ref, p_ref = rest[pg + 1:]
    b = pl.program_id(0)
    g = pl.program_id(1)
    ps = pages[0].shape[-1]
    past = n_pages * ps
    tot = past + LANES
    n_slab = tot // LANES
    rows = group * t_new
    rb = pl.multiple_of((b % group) * t_new, t_new)
    qi = qi_ref[...]
    w = w_ref[...]

    def scores(kt):
        s = jnp.dot(qi, kt.astype(BF16), preferred_element_type=F32)
        r = jnp.maximum(s, 0.0) * w
        return _tree_sum(r[h * t_new:(h + 1) * t_new] for h in range(N_IDX_HEADS))

    kt = jnp.concatenate([p[...] for p in pages], axis=-1)
    c0 = pl.multiple_of(g * (pg * ps), pg * ps)
    it_ref[pl.ds(rb, t_new), pl.ds(c0, pg * ps)] = scores(kt)
    last_g = g == pl.num_programs(1) - 1

    @pl.when(last_g)
    def _():
        sn = scores(kinew_ref[...].astype(F32))
        tq_i = lax.broadcasted_iota(I32, (t_new, LANES), 0)
        tk_i = lax.broadcasted_iota(I32, (t_new, LANES), 1)
        it_ref[pl.ds(rb, t_new), past:tot] = jnp.where(tk_i <= tq_i, sn, -jnp.inf)

    @pl.when(last_g & (b % group == group - 1))
    def _():
        lane_i = lax.broadcasted_iota(I32, (rows, LANES), 1)

        def count(pred):
            accs = [jnp.zeros((rows, LANES), F32) for _ in range(2)]
            for sl in range(n_slab):
                x = it_ref[:, sl * LANES:(sl + 1) * LANES]
                accs[sl % 2] = accs[sl % 2] + jnp.where(pred(x, sl * LANES), 1.0, 0.0)
            return jnp.sum(accs[0] + accs[1], axis=1, keepdims=True)

        def bis(_, st):
            lo, hi, c_lo, c_hi = st
            mid = _mid(lo, hi)
            midf = _key_to_float(mid)
            c = count(lambda x, c0_: x >= midf)
            ok = c >= n_sel
            return (jnp.where(ok, mid, lo), jnp.where(ok, hi, mid),
                    jnp.where(ok, c, c_lo), jnp.where(ok, c_hi, c))

        zero = jnp.zeros((rows, 1), F32)
        lo, _, c_ge, c_gt = lax.fori_loop(
            0, N_BISECT, bis,
            (jnp.full((rows, 1), KEY_LO, I32), jnp.full((rows, 1), KEY_HI, I32), zero, zero))
        tau = _key_to_float(lo)
        need = n_sel - c_gt
        p_ref[...] = jnp.full((rows, 1), tot, I32)

        @pl.when(jnp.max(c_ge) > n_sel)
        def _():
            def tie(_, lohi):
                plo, phi = lohi
                pm = (plo + phi) >> 1
                ok = count(lambda x, c0_: (x == tau) & (c0_ + lane_i <= pm)) >= need
                return jnp.where(ok, plo, pm), jnp.where(ok, pm, phi)
            n_it = int(math.ceil(math.log2(tot))) + 1
            _, phi = lax.fori_loop(0, n_it, tie, (jnp.full((rows, 1), -1, I32),
                                                  jnp.full((rows, 1), tot - 1, I32)))
            p_ref[...] = phi

        pcut = p_ref[...]
        for sl in range(n_slab):
            x = it_ref[:, sl * LANES:(sl + 1) * LANES]
            sel = (x > tau) | ((x == tau) & (sl * LANES + lane_i <= pcut))
            madd_ref[:, :, sl * LANES:(sl + 1) * LANES] = jnp.where(sel, 0.0, NEG).reshape(
                group, t_new, LANES)


def _sample_select(page_table, qi_s, w_s, kinew, kidxT, pg, n_sel, group):
    db, n_pages = page_table.shape
    ps = kidxT.shape[-1]
    t_new = qi_s.shape[1] // N_IDX_HEADS
    tot = n_pages * ps + LANES
    assert n_pages % pg == 0 and ps == LANES and db % group == 0

    def page_spec(u):
        return pl.BlockSpec((None, IDX_DIM, ps), lambda b, g, pt: (pt[b, g * pg + u], 0, 0))

    grid_spec = pltpu.PrefetchScalarGridSpec(
        num_scalar_prefetch=1,
        grid=(db, n_pages // pg),
        in_specs=[pl.BlockSpec((None,) + qi_s.shape[1:], lambda b, g, pt: (b, 0, 0)),
                  pl.BlockSpec((None,) + w_s.shape[1:], lambda b, g, pt: (b, 0, 0)),
                  pl.BlockSpec((None,) + kinew.shape[1:], lambda b, g, pt: (b, 0, 0))]
                 + [page_spec(u) for u in range(pg)],
        out_specs=pl.BlockSpec((group, t_new, tot), lambda b, g, pt: (b // group, 0, 0)),
        scratch_shapes=[pltpu.VMEM((group * t_new, tot), F32),
                        pltpu.VMEM((group * t_new, 1), I32)])
    return pl.pallas_call(
        functools.partial(_sidx_body, pg=pg, n_pages=n_pages, t_new=t_new, n_sel=n_sel,
                          group=group),
        grid_spec=grid_spec,
        out_shape=jax.ShapeDtypeStruct((db, t_new, tot), F32),
        compiler_params=pltpu.CompilerParams(dimension_semantics=("arbitrary", "arbitrary"),
                                             vmem_limit_bytes=VMEM_LIMIT),
        name="sample_select",
    )(page_table, qi_s, w_s, kinew, *([kidxT] * pg))


def _sattn_body(pt_ref, qbd_ref, madd_ref, maddn_ref, knew_ref, vnew_ref, bias_ref, *rest, pg):
    kpages = rest[:pg]
    vpages = rest[pg:2 * pg]
    o_ref = rest[2 * pg]
    m_ref, l_ref, acc_ref = rest[2 * pg + 1:]
    g = pl.program_id(1)
    last = pl.num_programs(1) - 1
    qbd = qbd_ref[...]
    ps = kpages[0].shape[-1]
    t_new = madd_ref.shape[0]
    hd = N_HEADS * HEAD_DIM

    @pl.when(g == 0)
    def _():
        m_ref[...] = jnp.full(m_ref.shape, NEG, F32)
        l_ref[...] = jnp.zeros(l_ref.shape, F32)
        acc_ref[...] = jnp.zeros(acc_ref.shape, F32)

    def flash(s, vt):
        m_old = m_ref[...]
        m_new = jnp.maximum(m_old, jnp.max(s, axis=-1, keepdims=True))
        p = jnp.exp2(s - m_new)
        alpha = jnp.exp2(m_old - m_new)
        l_ref[...] = alpha * l_ref[...] + jnp.sum(p, axis=-1, keepdims=True)
        pv = lax.dot_general(p.astype(BF16), vt, (((1,), (1,)), ((), ())),
                             preferred_element_type=F32)
        acc_ref[...] = alpha * acc_ref[...] + pv
        m_ref[...] = m_new

    def add_rows(s, add):
        n = s.shape[-1]
        return (s.reshape(N_HEADS, t_new, n) + add[None]).reshape(N_HEADS * t_new, n)

    kt = jnp.concatenate([kp[...].reshape(hd, ps) for kp in kpages], axis=-1).astype(BF16)
    vt = jnp.concatenate([vp[...].reshape(hd, ps) for vp in vpages], axis=-1).astype(BF16)
    s = jnp.dot(qbd, kt, preferred_element_type=F32)
    s = add_rows(s, madd_ref[...])
    is_last = jnp.where(g == last, 1.0, 0.0)
    tail = s[:, (pg - 1) * ps:] + is_last * bias_ref[0].reshape(N_HEADS * t_new, ps)
    s = jnp.concatenate([s[:, :(pg - 1) * ps], tail], axis=-1)
    flash(s, vt)

    @pl.when(g == last)
    def _():
        sn = jnp.dot(qbd, knew_ref[...], preferred_element_type=F32)
        sn = add_rows(sn + bias_ref[1].reshape(N_HEADS * t_new, LANES), maddn_ref[...])
        flash(sn, vnew_ref[...])
        out = acc_ref[...] / l_ref[...]
        for h in range(N_HEADS):
            o_ref[h] = out[h * t_new:(h + 1) * t_new, h * HEAD_DIM:(h + 1) * HEAD_DIM]


def _sample_attention(page_table, qbd, madd, knew, vnew, bias_s, cache_kT, cache_vT, pg):
    db, n_pages = page_table.shape
    ps = cache_kT.shape[-1]
    t_new = madd.shape[1]
    assert n_pages % pg == 0

    def page_spec(u):
        return pl.BlockSpec((None, N_HEADS, HEAD_DIM, ps),
                            lambda b, g, pt: (pt[b, g * pg + u], 0, 0, 0))

    per_b = lambda b, g, pt: (b, 0, 0)
    grid_spec = pltpu.PrefetchScalarGridSpec(
        num_scalar_prefetch=1,
        grid=(db, n_pages // pg),
        in_specs=[pl.BlockSpec((None,) + qbd.shape[1:], per_b),
                  pl.BlockSpec((None, t_new, pg * ps), lambda b, g, pt: (b, 0, g)),
                  pl.BlockSpec((None, t_new, LANES), lambda b, g, pt: (b, 0, n_pages * ps // LANES)),
                  pl.BlockSpec((None,) + knew.shape[1:], per_b),
                  pl.BlockSpec((None,) + vnew.shape[1:], per_b),
                  pl.BlockSpec(bias_s.shape, lambda b, g, pt: (0, 0, 0, 0))]
                 + [page_spec(u) for u in range(pg)] * 2,
        out_specs=pl.BlockSpec((None, N_HEADS, t_new, HEAD_DIM), lambda b, g, pt: (b, 0, 0, 0)),
        scratch_shapes=[pltpu.VMEM((N_HEADS * t_new, 1), F32),
                        pltpu.VMEM((N_HEADS * t_new, 1), F32),
                        pltpu.VMEM((N_HEADS * t_new, N_HEADS * HEAD_DIM), F32)])
    return pl.pallas_call(
        functools.partial(_sattn_body, pg=pg),
        grid_spec=grid_spec,
        out_shape=jax.ShapeDtypeStruct((db, N_HEADS, t_new, HEAD_DIM), F32),
        compiler_params=pltpu.CompilerParams(dimension_semantics=("arbitrary", "arbitrary"),
                                             vmem_limit_bytes=VMEM_LIMIT),
        name="sample_attention",
    )(page_table, qbd, madd, madd, knew, vnew, bias_s, *([cache_kT] * pg), *([cache_vT] * pg))


def _outmlp_body(x_ref, a_ref, bg_ref, u_ref, prev_ref, cw_ref, wo_ref, gm_ref, wu_ref, wd_ref,
                 gf_ref, y_ref, *, seq_len, ff_chunk):
    tm = x_ref.shape[0]
    u = u_ref[...]
    w0 = cw_ref[0:1, :]
    w1 = cw_ref[1:2, :]
    w2 = cw_ref[2:3, :]
    if seq_len >= tm:
        first = (pl.program_id(0) % (seq_len // tm)) == 0
        halo = prev_ref[...] * jnp.where(first, 0.0, 1.0)
        row = lax.broadcasted_iota(I32, u.shape, 0)
        um1 = jnp.where(row == 0, halo[7:8, :], pltpu.roll(u, 1, 0))
        um2 = jnp.where(row == 0, halo[6:7, :],
                        jnp.where(row == 1, halo[7:8, :], pltpu.roll(u, 2, 0)))
    else:
        nseq = tm // seq_len
        u3 = u.reshape(nseq, seq_len, u.shape[-1])
        up = jnp.concatenate([prev_ref[...], u3], axis=1)
        um1 = up[:, 1:1 + seq_len].reshape(u.shape)
        um2 = up[:, 0:seq_len].reshape(u.shape)
    y = um2 * w0
    y = y + um1 * w1
    y = y + u * w2
    b = (bg_ref[...] * y).astype(BF16)
    ab = jnp.concatenate([a_ref[...], b], axis=-1)
    x1 = x_ref[...] + jnp.dot(ab, wo_ref[...], preferred_element_type=F32)
    ms = jnp.mean(x1 * x1, axis=-1, keepdims=True)
    hn = ((x1 * lax.rsqrt(ms + EPS)) * gm_ref[...]).astype(BF16)
    acc = jnp.zeros(x1.shape, F32)
    d_ff = wu_ref.shape[1]
    for c in range(d_ff // ff_chunk):
        sl = slice(c * ff_chunk, (c + 1) * ff_chunk)
        up_c = jnp.dot(hn, wu_ref[:, sl], preferred_element_type=F32)
        r = jnp.maximum(up_c, 0.0)
        acc = acc + jnp.dot((r * r).astype(BF16), wd_ref[sl, :], preferred_element_type=F32)
    x2 = x1 + acc
    ms2 = jnp.mean(x2 * x2, axis=-1, keepdims=True)
    y_ref[...] = (x2 * lax.rsqrt(ms2 + EPS)) * gf_ref[...]


def _out_mlp(x2d, a, bg, u, prev, conv_w, wo, g_mlp, wu, wd, g_final, tm, seq_len):
    n, d = x2d.shape
    dc = bg.shape[1]
    row = lambda i: (i, 0)
    c2 = lambda i: (0, 0)
    if seq_len >= tm:
        prev_spec = pl.BlockSpec((SUBLANES, dc),
                                 lambda i: (jnp.maximum(i * (tm // SUBLANES) - 1, 0), 0))
        prev_arg = u
    else:
        nseq = tm // seq_len
        prev_spec = pl.BlockSpec((nseq,) + prev.shape[1:], lambda i: (i, 0, 0))
        prev_arg = prev
    single = dict(pipeline_mode=pl.Buffered(1))
    return pl.pallas_call(
        functools.partial(_outmlp_body, seq_len=seq_len, ff_chunk=1024),
        grid=(n // tm,),
        in_specs=[pl.BlockSpec((tm, d), row),
                  pl.BlockSpec((tm, a.shape[1]), row),
                  pl.BlockSpec((tm, dc), row),
                  pl.BlockSpec((tm, dc), row),
                  prev_spec,
                  pl.BlockSpec(conv_w.shape, c2),
                  pl.BlockSpec(wo.shape, c2, **single),
                  pl.BlockSpec((1, d), c2),
                  pl.BlockSpec(wu.shape, c2, **single),
                  pl.BlockSpec(wd.shape, c2, **single),
                  pl.BlockSpec((1, d), c2)],
        out_specs=pl.BlockSpec((tm, d), row),
        out_shape=jax.ShapeDtypeStruct((n, d), F32),
        compiler_params=pltpu.CompilerParams(dimension_semantics=("arbitrary",),
                                             vmem_limit_bytes=VMEM_LIMIT),
        name="out_mlp",
    )(x2d, a, bg, u, prev_arg, conv_w, wo, g_mlp, wu, wd, g_final)


def _pick_tile(n, pref):
    t = min(pref, n)
    while n % t:
        t //= 2
    return t


def kernel(x_prompt, x_sample, cache_k, cache_v, cache_kidx, state_conv, page_table, rel_bias,
           g_mix, w_in, conv_w, w_out, g_mlp, w_up, w_down, g_final):
    depth = w_in.shape[0]
    assert depth == 1, "single-layer step"
    batch, seq, d_model = x_prompt.shape
    db, t_new, _ = x_sample.shape
    n_pages = page_table.shape[1]
    ps = cache_k.shape[2]
    past = n_pages * ps
    assert ps == LANES and t_new == SUBLANES

    tq = _pick_tile(seq, 256)
    n_sel_p = min(TOPK_MAX, seq // 4)
    n_sel_s = min(TOPK_MAX, (past + t_new) // 4)
    assert _far_bucket_is_constant(tq + 1, seq) and _far_bucket_is_constant(ps + 1, past + t_new)

    w = w_in[0]
    cq, ck_, cv, cqi, cki, cwi, cbg, ccg, ch = np.cumsum(
        [0, D_ATTN, D_ATTN, D_ATTN, N_IDX_HEADS * IDX_DIM, IDX_DIM, N_IDX_HEADS, D_ATTN, D_ATTN])
    end = ch + D_ATTN
    wt = jnp.pad(w[:, cq:cbg].T, ((0, T_END - cbg), (0, 0))).astype(BF16)
    wn = jnp.concatenate(
        [w[:, ck_:cv], w[:, cbg:end], jnp.pad(w[:, cki:cwi], ((0, 0), (0, LANES - IDX_DIM)))],
        axis=1).astype(BF16)
    wo = w_out[0].astype(BF16)
    wu = w_up[0].astype(BF16)
    wd = w_down[0].astype(BF16)
    gmix = g_mix[0][None]
    gmlp = g_mlp[0][None]
    gfin = g_final[None]
    cw = conv_w[0]

    n_p = batch * seq
    (qT, kT, vT, vTb, qiT, kiT, wiT, kn, kin, bg, u) = _in_proj(
        x_prompt, gmix, wn, wt, _pick_tile(seq, 512))
    bias_p = _bias_tables(rel_bias, tq, tq, (0, tq), -1, 1)
    a_p = _prompt_attention(qT, qiT, wiT, kn, kin, vTb, bias_p, tq, n_sel_p)
    y_prompt = _out_mlp(x_prompt.reshape(n_p, d_model), a_p.reshape(n_p, D_ATTN),
                        bg.reshape(n_p, D_ATTN), u.reshape(n_p, D_ATTN), None, cw, wo, gmlp, wu, wd,
                        gfin, _pick_tile(seq, 512), seq).reshape(batch, seq, d_model)

    def heads_out(t):
        b_, _, s_ = t.shape
        return t.reshape(b_, N_HEADS, HEAD_DIM, s_).transpose(0, 3, 1, 2)[None]

    k_prompt = heads_out(kT)
    v_prompt = heads_out(vT)
    kidx_prompt = kiT.transpose(0, 2, 1)[None]
    conv_prompt = u[:, seq - (CONV_WIDTH - 1):][None]

    ns = db * t_new
    (qTs, kTs, vTs, _, qiTs, kiTs, wiTs, _, _, bgs, us) = [
        t[0] for t in _in_proj(x_sample.reshape(1, ns, d_model), gmix, wn, wt, ns)]
    qi_s = qiTs.reshape(N_IDX_HEADS, IDX_DIM, db, t_new).transpose(2, 0, 3, 1).reshape(
        db, N_IDX_HEADS * t_new, IDX_DIM)
    w_s = wiTs.reshape(N_IDX_HEADS, db, t_new).transpose(1, 0, 2).reshape(db, N_IDX_HEADS * t_new, 1)
    kinew = jnp.pad(kiTs.reshape(IDX_DIM, db, t_new).transpose(1, 0, 2),
                    ((0, 0), (0, 0), (0, LANES - t_new))).astype(BF16)
    q_s = qTs.reshape(N_HEADS, HEAD_DIM, db, t_new).transpose(2, 0, 3, 1)
    eye = jnp.eye(N_HEADS, dtype=q_s.dtype)
    qbd = (q_s[:, :, :, None, :] * eye[None, :, None, :, None]).reshape(
        db, N_HEADS * t_new, N_HEADS * HEAD_DIM)
    pad_new = lambda t: jnp.pad(t.reshape(D_ATTN, db, t_new).transpose(1, 0, 2),
                                ((0, 0), (0, 0), (0, LANES - t_new))).astype(BF16)
    knew = pad_new(kTs)
    vnew = pad_new(vTs)
    kidxT = cache_kidx[0].transpose(0, 2, 1)
    cache_kT = cache_k[0].transpose(0, 2, 3, 1)
    cache_vT = cache_v[0].transpose(0, 2, 3, 1)

    madd = _sample_select(page_table, qi_s, w_s, kinew, kidxT, _pick_tile(n_pages, 128), n_sel_s,
                          _pick_tile(db, 4))
    bias_s = _bias_tables(rel_bias, t_new, LANES, (ps, 0), 1, -1)
    o_s = _sample_attention(page_table, qbd, madd, knew, vnew, bias_s, cache_kT, cache_vT,
                            _pick_tile(n_pages, 32))
    a_s = o_s.transpose(0, 2, 1, 3).reshape(ns, D_ATTN).astype(BF16)
    y_sample = _out_mlp(x_sample.reshape(ns, d_model), a_s, bgs, us, state_conv[0], cw, wo, gmlp,
                        wu, wd, gfin, ns, t_new).reshape(db, t_new, d_model)

    def heads_out_s(t):
        return t.reshape(N_HEADS, HEAD_DIM, db, t_new).transpose(2, 3, 0, 1)[None]

    k_sample = heads_out_s(kTs)
    v_sample = heads_out_s(vTs)
    kidx_sample = kiTs.reshape(IDX_DIM, db, t_new).transpose(1, 2, 0)[None]
    conv_sample = us.reshape(db, t_new, D_ATTN)[:, t_new - (CONV_WIDTH - 1):][None]

    return (y_prompt, y_sample, k_prompt, v_prompt, kidx_prompt, conv_prompt,
            k_sample, v_sample, kidx_sample, conv_sample)
```
